```python
import math
import jax
import jax.numpy as jnp
from jax import lax
import numpy as np

D_MODEL = 2048
BATCH = 4
SEQ = 4096
DEPTH = 1
DEC_BATCH = 32
DEC_SEQ = 1
PAST_LEN = 16384
PAGE_SIZE = 128

D_ATTN = D_MODEL // 2
HEAD_DIM = 128
N_HEADS = D_ATTN // HEAD_DIM
ROT_DIMS = HEAD_DIM // 4
ROPE_THETA = 500000.0
MOBA_BLOCK = 256
MOBA_TOPK = 3
Q_CHUNK = 64
D_SSM = D_MODEL // 2
GROUP_CH = 16
N_GROUPS = D_SSM // GROUP_CH
STATE_DIM = 64
N_EXPERTS = 32
TOP_K = 4
D_FF = D_MODEL
SWIGLU_LIMIT = 7.0
SWIGLU_ALPHA = 1.702
MOE_MAX_BLOCK = 256
LN_EPS = 1e-5
DEEPNORM_ALPHA = (2 * DEPTH) ** 0.25
DEEPNORM_BETA = (8 * DEPTH) ** -0.25
SPLITS = (D_ATTN, 2 * D_ATTN, 3 * D_ATTN, 3 * D_ATTN + D_SSM, 3 * D_ATTN + D_SSM + D_MODEL)
D_IN = 3 * D_ATTN + D_SSM + 2 * D_MODEL

kernel_name = 'hybrid_s5_moba_moe_step'


def _layernorm(x, g, b):
    xf = x.astype(jnp.float32)
    mu = jnp.mean(xf, axis=-1, keepdims=True)
    xc = xf - mu
    var = jnp.mean(xc * xc, axis=-1, keepdims=True)
    return (xc * lax.rsqrt(var + LN_EPS) * g.astype(jnp.float32) + b.astype(jnp.float32)).astype(x.dtype)


def _rope(x, pos):
    half = ROT_DIMS // 2
    inv = ROPE_THETA ** (-jnp.arange(half, dtype=jnp.float32) * 2.0 / ROT_DIMS)
    ang = pos.astype(jnp.float32)[:, None] * inv[None, :]
    cos = jnp.cos(ang)[None, :, None, :]
    sin = jnp.sin(ang)[None, :, None, :]
    xf = x.astype(jnp.float32)
    x1 = xf[..., :half]
    x2 = xf[..., half:ROT_DIMS]
    out = jnp.concatenate([x1 * cos - x2 * sin, x2 * cos + x1 * sin, xf[..., ROT_DIMS:]], axis=-1)
    return out.astype(x.dtype)


def _mixer_inputs(x, pos, w_in):
    n, s, _ = x.shape
    p = x @ w_in
    q, k, v, u, ga, gb = jnp.split(p, SPLITS, axis=-1)
    q = _rope(q.reshape(n, s, N_HEADS, HEAD_DIM), pos)
    k = _rope(k.reshape(n, s, N_HEADS, HEAD_DIM), pos)
    v = v.reshape(n, s, N_HEADS, HEAD_DIM)
    return q, k, v, u, ga, gb


def _head_index(ndim):
    return jnp.arange(N_HEADS).reshape((N_HEADS,) + (1,) * (ndim - 1))


def _moba_attend(qc, q_pos, kmean, fetch, loc_rows):
    nq, h, d = qc.shape
    nb = kmean.shape[1]
    n_sel = min(MOBA_TOPK, nb)
    scale = 1.0 / math.sqrt(HEAD_DIM)
    own = q_pos // MOBA_BLOCK
    gate = jnp.einsum('qhd,hnd->hqn', qc.astype(jnp.float32), kmean)
    gate = jnp.where(jnp.arange(nb)[None, None, :] < own[None, :, None], gate, -jnp.inf)
    _, sel = lax.top_k(gate, n_sel)
    sel_ok = sel < own[None, :, None]
    rows = sel[..., None] * MOBA_BLOCK + jnp.arange(MOBA_BLOCK)
    k_sel, v_sel = fetch(rows)
    s_sel = jnp.einsum('qhd,hqkjd->hqkj', qc, k_sel, preferred_element_type=jnp.float32) * scale
    n_s = n_sel * MOBA_BLOCK
    s_sel = jnp.where(sel_ok[..., None], s_sel, -jnp.inf).reshape(h, nq, n_s)
    n_loc = loc_rows.shape[0]
    k_loc, v_loc = fetch(jnp.broadcast_to(loc_rows, (N_HEADS, n_loc)))
    s_loc = jnp.einsum('qhd,hld->hql', qc, k_loc, preferred_element_type=jnp.float32) * scale
    own_start = own * MOBA_BLOCK
    vis = (loc_rows[None, :] >= own_start[:, None]) & (loc_rows[None, :] <= q_pos[:, None])
    s_loc = jnp.where(vis[None], s_loc, -jnp.inf)
    p = jax.nn.softmax(jnp.concatenate([s_sel, s_loc], axis=-1), axis=-1)
    p_sel = p[..., :n_s].astype(v_sel.dtype)
    p_loc = p[..., n_s:].astype(v_loc.dtype)
    out = (jnp.einsum('hqj,hqjd->qhd', p_sel, v_sel.reshape(h, nq, n_s, d), preferred_element_type=jnp.float32)
           + jnp.einsum('hql,hld->qhd', p_loc, v_loc, preferred_element_type=jnp.float32))
    return out.astype(qc.dtype)


def _moba_prefill(q, k, v):
    n, s, h, d = q.shape
    nb = -(-s // MOBA_BLOCK)
    pad = nb * MOBA_BLOCK - s

    def one_seq(args):
        qs, ks, vs = args
        kp = jnp.pad(ks, ((0, pad), (0, 0), (0, 0)))
        vp = jnp.pad(vs, ((0, pad), (0, 0), (0, 0)))
        kmean = kp.reshape(nb, MOBA_BLOCK, h, d).astype(jnp.float32).mean(axis=1).transpose(1, 0, 2)

        def fetch(rows):
            hi = _head_index(rows.ndim)
            return kp[rows, hi], vp[rows, hi]

        def chunk(ci):
            q0 = ci * Q_CHUNK
            qc = lax.dynamic_slice_in_dim(qs, q0, Q_CHUNK, axis=0)
            q_pos = q0 + jnp.arange(Q_CHUNK)
            loc_rows = (q0 // MOBA_BLOCK) * MOBA_BLOCK + jnp.arange(MOBA_BLOCK)
            return _moba_attend(qc, q_pos, kmean, fetch, loc_rows)

        out = lax.map(chunk, jnp.arange(s // Q_CHUNK))
        return out.reshape(s, h, d)

    return lax.map(one_seq, (q, k, v))


def _moba_decode(q, k_new, v_new, k_pool, v_pool, page_table):
    n, t, h, d = q.shape
    page = k_pool.shape[1]
    past = page_table.shape[1] * page
    total = past + t
    nb = -(-total // MOBA_BLOCK)
    pad = nb * MOBA_BLOCK - total
    start0 = (past // MOBA_BLOCK) * MOBA_BLOCK
    q_pos = past + jnp.arange(t)
    loc_rows = jnp.arange(start0, total)

    def one_seq(args):
        qs, kn, vn, pt = args
        k_all = jnp.concatenate([k_pool[pt].reshape(past, h, d), kn], axis=0)
        k_all = jnp.pad(k_all, ((0, pad), (0, 0), (0, 0)))
        kmean = k_all.reshape(nb, MOBA_BLOCK, h, d).astype(jnp.float32).mean(axis=1).transpose(1, 0, 2)

        def fetch(rows):
            hi = _head_index(rows.ndim)
            in_past = (rows < past)[..., None]
            rp = jnp.minimum(rows, past - 1)
            phys = pt[rp // page]
            off = rp % page
            rn = jnp.clip(rows - past, 0, t - 1)
            kk = jnp.where(in_past, k_pool[phys, off, hi], kn[rn, hi])
            vv = jnp.where(in_past, v_pool[phys, off, hi], vn[rn, hi])
            return kk, vv

        return _moba_attend(qs, q_pos, kmean, fetch, loc_rows)

    return lax.map(one_seq, (q, k_new, v_new, page_table))


def _s5_scan(u, h0_re, h0_im, lam_re, lam_im, log_dt, b_re, b_im, c_re, c_im, d_skip):
    f32 = jnp.float32
    n, s, _ = u.shape
    ug = u.reshape(n, s, N_GROUPS, GROUP_CH).astype(f32)
    lam_re = lam_re.astype(f32)
    lam_im = lam_im.astype(f32)
    dt = jnp.exp(log_dt.astype(f32))[:, None]
    mag = jnp.exp(lam_re * dt)
    ang = lam_im * dt
    a_re = mag * jnp.cos(ang)
    a_im = mag * jnp.sin(ang)
    den = lam_re * lam_re + lam_im * lam_im
    nr = a_re - 1.0
    f_re = (nr * lam_re + a_im * lam_im) / den
    f_im = (a_im * lam_re - nr * lam_im) / den
    br = b_re.astype(f32)
    bi = b_im.astype(f32)
    bb_re = f_re[..., None] * br - f_im[..., None] * bi
    bb_im = f_re[..., None] * bi + f_im[..., None] * br
    x_re = jnp.einsum('gpc,nsgc->nsgp', bb_re, ug)
    x_im = jnp.einsum('gpc,nsgc->nsgp', bb_im, ug)
    h0r = h0_re.astype(f32)
    h0i = h0_im.astype(f32)
    x_re = x_re.at[:, 0].add(a_re * h0r - a_im * h0i)
    x_im = x_im.at[:, 0].add(a_re * h0i + a_im * h0r)
    A_re = jnp.broadcast_to(a_re, x_re.shape)
    A_im = jnp.broadcast_to(a_im, x_im.shape)

    def combine(e1, e2):
        a1r, a1i, b1r, b1i = e1
        a2r, a2i, b2r, b2i = e2
        return (a2r * a1r - a2i * a1i, a2r * a1i + a2i * a1r,
                a2r * b1r - a2i * b1i + b2r, a2r * b1i + a2i * b1r + b2i)

    _, _, h_re, h_im = lax.associative_scan(combine, (A_re, A_im, x_re, x_im), axis=1)
    y = (jnp.einsum('gcp,nsgp->nsgc', c_re.astype(f32), h_re)
         - jnp.einsum('gcp,nsgp->nsgc', c_im.astype(f32), h_im)
         + d_skip.astype(f32).reshape(N_GROUPS, GROUP_CH) * ug)
    return y.reshape(n, s, D_SSM), h_re[:, -1], h_im[:, -1]


def _moe_block_rows(tk):
    blk = 8
    while blk < MOE_MAX_BLOCK and blk * N_EXPERTS < tk:
        blk *= 2
    return blk


def _moe(x, w_router, b_router, w_gate, b_gate, w_up, b_up, w_down, b_down):
    shp = x.shape
    x2 = x.reshape(-1, shp[-1])
    t = x2.shape[0]
    tk = t * TOP_K
    logits = (x2 @ w_router).astype(jnp.float32) + b_router.astype(jnp.float32)
    top_v, top_e = lax.top_k(logits, TOP_K)
    gates = jax.nn.softmax(top_v, axis=-1)
    flat_e = top_e.reshape(-1)
    order = jnp.argsort(flat_e, stable=True)
    s_e = flat_e[order]
    s_tok = order // TOP_K
    s_gate = gates.reshape(-1)[order]
    counts = jnp.bincount(flat_e, length=N_EXPERTS)
    blk = _moe_block_rows(tk)
    padded = ((counts + blk - 1) // blk) * blk
    pad_end = jnp.cumsum(padded)
    pad_start = pad_end - padded
    start = jnp.cumsum(counts) - counts
    dest = pad_start[s_e] + jnp.arange(tk) - start[s_e]
    n_blocks = -(-tk // blk) + N_EXPERTS
    n_rows = n_blocks * blk
    row_tok = jnp.full((n_rows,), t, jnp.int32).at[dest].set(s_tok.astype(jnp.int32))
    row_gate = jnp.zeros((n_rows,), jnp.float32).at[dest].set(s_gate)
    blk_e = jnp.minimum(jnp.searchsorted(pad_end, jnp.arange(n_blocks) * blk, side='right'), N_EXPERTS - 1)
    xp = jnp.concatenate([x2, jnp.zeros((1, x2.shape[1]), x2.dtype)], axis=0)
    xb = xp[row_tok].reshape(n_blocks, blk, x2.shape[1])

    def expert_block(args):
        xe, e = args
        g = xe @ w_gate[e] + b_gate[e]
        up = xe @ w_up[e] + b_up[e]
        g = jnp.minimum(g, SWIGLU_LIMIT)
        up = jnp.clip(up, -SWIGLU_LIMIT, SWIGLU_LIMIT)
        act = (up + 1.0) * (g * jax.nn.sigmoid(SWIGLU_ALPHA * g))
        return act @ w_down[e] + b_down[e]

    yb = lax.map(expert_block, (xb, blk_e)).reshape(n_rows, -1)
    y = jnp.zeros((t + 1, x2.shape[1]), jnp.float32).at[row_tok].add(yb.astype(jnp.float32) * row_gate[:, None])
    return y[:t].reshape(shp).astype(x.dtype)


def _layer_out(x, y_ssm, attn, ga, gb, w_glu_val, w_glu_gate, w_attn_up, w_o, ln1_g, ln1_b,
               w_router, b_router, w_gate, b_gate, w_up, b_up, w_down, b_down, ln2_g, ln2_b):
    n, s, _ = x.shape
    z = jax.nn.gelu(y_ssm.astype(x.dtype))
    br_ssm = (z @ w_glu_val) * jax.nn.sigmoid(z @ w_glu_gate)
    br_attn = attn.reshape(n, s, D_ATTN) @ w_attn_up
    mix = jax.nn.sigmoid(ga) * br_ssm + jax.nn.sigmoid(gb) * br_attn
    h = _layernorm(DEEPNORM_ALPHA * x + mix @ w_o, ln1_g, ln1_b)
    ff = _moe(h, w_router, b_router, w_gate, b_gate, w_up, b_up, w_down, b_down)
    return _layernorm(DEEPNORM_ALPHA * h + ff, ln2_g, ln2_b)


def setup_inputs(seed: int = 0) -> dict:
    key = jax.random.key(seed)
    ks = iter(jax.random.split(key, 48))
    f32 = jnp.float32

    def nrm(shape, scale):
        return jax.random.normal(next(ks), shape, f32) * scale

    n_pages = PAST_LEN // PAGE_SIZE
    n_used = DEC_BATCH * n_pages
    n_phys = n_used + n_used // 4
    perm = jax.random.permutation(next(ks), n_phys)
    page_table = perm[:n_used].reshape(DEC_BATCH, n_pages).astype(jnp.int32)
    lam_im0 = jnp.pi * jnp.arange(STATE_DIM, dtype=f32)
    return {
        'x_prompt': nrm((BATCH, SEQ, D_MODEL), 1.0),
        'x_sample': nrm((DEC_BATCH, DEC_SEQ, D_MODEL), 1.0),
        'cache_k': nrm((DEPTH, n_phys, PAGE_SIZE, N_HEADS, HEAD_DIM), 1.0),
        'cache_v': nrm((DEPTH, n_phys, PAGE_SIZE, N_HEADS, HEAD_DIM), 1.0),
        'page_table': page_table,
        'state_ssm_re': nrm((DEPTH, DEC_BATCH, N_GROUPS, STATE_DIM), 0.1),
        'state_ssm_im': nrm((DEPTH, DEC_BATCH, N_GROUPS, STATE_DIM), 0.1),
        'w_in': nrm((DEPTH, D_MODEL, D_IN), D_MODEL ** -0.5),
        'ssm_lambda_re': -0.5 + nrm((DEPTH, N_GROUPS, STATE_DIM), 0.01),
        'ssm_lambda_im': lam_im0 + nrm((DEPTH, N_GROUPS, STATE_DIM), 0.01),
        'ssm_log_dt': jax.random.uniform(next(ks), (DEPTH, N_GROUPS), f32, math.log(0.001), math.log(0.1)),
        'ssm_b_re': nrm((DEPTH, N_GROUPS, STATE_DIM, GROUP_CH), (2 * GROUP_CH) ** -0.5),
        'ssm_b_im': nrm((DEPTH, N_GROUPS, STATE_DIM, GROUP_CH), (2 * GROUP_CH) ** -0.5),
        'ssm_c_re': nrm((DEPTH, N_GROUPS, GROUP_CH, STATE_DIM), STATE_DIM ** -0.5),
        'ssm_c_im': nrm((DEPTH, N_GROUPS, GROUP_CH, STATE_DIM), STATE_DIM ** -0.5),
        'ssm_d': nrm((DEPTH, D_SSM), 1.0),
        'w_glu_val': nrm((DEPTH, D_SSM, D_MODEL), D_SSM ** -0.5),
        'w_glu_gate': nrm((DEPTH, D_SSM, D_MODEL), D_SSM ** -0.5),
        'w_attn_up': nrm((DEPTH, D_ATTN, D_MODEL), D_ATTN ** -0.5),
        'w_o': nrm((DEPTH, D_MODEL, D_MODEL), DEEPNORM_BETA * D_MODEL ** -0.5),
        'ln1_g': 1.0 + nrm((DEPTH, D_MODEL), 0.02),
        'ln1_b': nrm((DEPTH, D_MODEL), 0.02),
        'w_router': nrm((DEPTH, D_MODEL, N_EXPERTS), D_MODEL ** -0.5),
        'b_router': nrm((DEPTH, N_EXPERTS), 0.01),
        'w_gate': nrm((DEPTH, N_EXPERTS, D_MODEL, D_FF), D_MODEL ** -0.5),
        'b_gate': nrm((DEPTH, N_EXPERTS, D_FF), 0.02),
        'w_up': nrm((DEPTH, N_EXPERTS, D_MODEL, D_FF), D_MODEL ** -0.5),
        'b_up': nrm((DEPTH, N_EXPERTS, D_FF), 0.02),
        'w_down': nrm((DEPTH, N_EXPERTS, D_FF, D_MODEL), DEEPNORM_BETA * D_FF ** -0.5),
        'b_down': nrm((DEPTH, N_EXPERTS, D_MODEL), 0.02),
        'ln2_g': 1.0 + nrm((DEPTH, D_MODEL), 0.02),
        'ln2_b': nrm((DEPTH, D_MODEL), 0.02),
    }


def reference(x_prompt, x_sample, cache_k, cache_v, page_table, state_ssm_re, state_ssm_im,
              w_in, ssm_lambda_re, ssm_lambda_im, ssm_log_dt, ssm_b_re, ssm_b_im, ssm_c_re, ssm_c_im, ssm_d,
              w_glu_val, w_glu_gate, w_attn_up, w_o, ln1_g, ln1_b,
              w_router, b_router, w_gate, b_gate, w_up, b_up, w_down, b_down, ln2_g, ln2_b):
    past = page_table.shape[1] * cache_k.shape[2]
    pos_p = jnp.arange(x_prompt.shape[1])
    pos_s = past + jnp.arange(x_sample.shape[1])
    hp = x_prompt
    hs = x_sample
    k_p, v_p, sr_p, si_p = [], [], [], []
    k_s, v_s, sr_s, si_s = [], [], [], []
    for l in range(DEPTH):
        ssm_p = (ssm_lambda_re[l], ssm_lambda_im[l], ssm_log_dt[l], ssm_b_re[l], ssm_b_im[l],
                 ssm_c_re[l], ssm_c_im[l], ssm_d[l])
        out_p = (w_glu_val[l], w_glu_gate[l], w_attn_up[l], w_o[l], ln1_g[l], ln1_b[l],
                 w_router[l], b_router[l], w_gate[l], b_gate[l], w_up[l], b_up[l], w_down[l], b_down[l],
                 ln2_g[l], ln2_b[l])
        q1, k1, v1, u1, ga1, gb1 = _mixer_inputs(hp, pos_p, w_in[l])
        attn1 = _moba_prefill(q1, k1, v1)
        zero = jnp.zeros((hp.shape[0], N_GROUPS, STATE_DIM), jnp.float32)
        y1, hr1, hi1 = _s5_scan(u1, zero, zero, *ssm_p)
        q2, k2, v2, u2, ga2, gb2 = _mixer_inputs(hs, pos_s, w_in[l])
        attn2 = _moba_decode(q2, k2, v2, cache_k[l], cache_v[l], page_table)
        y2, hr2, hi2 = _s5_scan(u2, state_ssm_re[l], state_ssm_im[l], *ssm_p)
        hp = _layer_out(hp, y1, attn1, ga1, gb1, *out_p)
        hs = _layer_out(hs, y2, attn2, ga2, gb2, *out_p)
        k_p.append(k1)
        v_p.append(v1)
        sr_p.append(hr1)
        si_p.append(hi1)
        k_s.append(k2)
        v_s.append(v2)
        sr_s.append(hr2)
        si_s.append(hi2)
    return (hp, hs, jnp.stack(k_p), jnp.stack(v_p), jnp.stack(sr_p), jnp.stack(si_p),
            jnp.stack(k_s), jnp.stack(v_s), jnp.stack(sr_s), jnp.stack(si_s))
```

```python
import functools
import math

import jax
import jax.numpy as jnp
from jax import lax
from jax.experimental import pallas as pl
from jax.experimental.pallas import tpu as pltpu

F32, BF16, I32 = jnp.float32, jnp.bfloat16, jnp.int32

ROPE_THETA = 500000.0
ROT_FRACTION = 4
MOBA_BLOCK = 256
MOBA_TOPK = 3
TOP_K = 4
SWIGLU_LIMIT = 7.0
SWIGLU_ALPHA = 1.702
LN_EPS = 1e-5
MOE_BLOCK = 256

LANES = 128
SUBLANES = 8
VMEM_LIMIT_BYTES = 56 * 1024 * 1024

NEG_BIG = -1e30


def _cparams(n_axes):
    return pltpu.CompilerParams(dimension_semantics=("arbitrary",) * n_axes,
                                vmem_limit_bytes=VMEM_LIMIT_BYTES)


def _sigmoid(x):
    return 1.0 / (1.0 + jnp.exp(-x))


def _gelu_tanh(x):
    c = math.sqrt(2.0 / math.pi)
    return x * (0.5 * (1.0 + jnp.tanh(c * (x + 0.044715 * (x * x * x)))))


def _layernorm(r, g, b):
    mu = jnp.mean(r, axis=-1, keepdims=True)
    xc = r - mu
    var = jnp.mean(xc * xc, axis=-1, keepdims=True)
    return xc * lax.rsqrt(var + LN_EPS) * g + b


def _proj_kernel(x_ref, w_ref, *rest, rope, rot_half, hd):
    acc = jnp.dot(x_ref[...], w_ref[...], preferred_element_type=F32)
    if not rope:
        (o_ref,) = rest
        o_ref[...] = acc.astype(o_ref.dtype)
        return
    cos_ref, sin_ref, o_ref = rest
    cos = cos_ref[...]
    sin = sin_ref[...]
    first = lax.broadcasted_iota(I32, cos.shape, 1) < rot_half
    for hh in range(acc.shape[1] // hd):
        xh = acc[:, hh * hd:(hh + 1) * hd]
        rot = jnp.where(first, pltpu.roll(xh, hd - rot_half, 1), pltpu.roll(xh, rot_half, 1))
        o_ref[:, hh * hd:(hh + 1) * hd] = (xh * cos + rot * sin).astype(o_ref.dtype)


def _proj(xb, wb, col0, ncols, out_dtype, *, tm, tn, rope_tabs=None, hd=LANES, name):
    t, k = xb.shape
    tm = min(tm, t)
    tn = min(tn, ncols)
    off = col0 // tn
    assert col0 % tn == 0 and ncols % tn == 0 and t % tm == 0
    in_specs = [pl.BlockSpec((tm, k), lambda i, j: (i, 0)),
                pl.BlockSpec((k, tn), lambda i, j: (0, j + off))]
    args = [xb, wb]
    if rope_tabs is not None:
        in_specs += [pl.BlockSpec((tm, hd), lambda i, j: (i, 0))] * 2
        args += list(rope_tabs)
    return pl.pallas_call(
        functools.partial(_proj_kernel, rope=rope_tabs is not None, rot_half=hd // ROT_FRACTION // 2, hd=hd),
        grid=(t // tm, ncols // tn),
        in_specs=in_specs,
        out_specs=pl.BlockSpec((tm, tn), lambda i, j: (i, j)),
        out_shape=jax.ShapeDtypeStruct((t, ncols), out_dtype),
        compiler_params=_cparams(2),
        name=name,
    )(*args)


def _rope_tables(pos, hd):
    rot = hd // ROT_FRACTION
    half = rot // 2
    inv = ROPE_THETA ** (-jnp.arange(half, dtype=F32) * 2.0 / rot)
    ang = pos.astype(F32)[:, None] * inv[None, :]
    cos, sin = jnp.cos(ang), jnp.sin(ang)
    t = pos.shape[0]
    cosf = jnp.concatenate([cos, cos, jnp.ones((t, hd - rot), F32)], axis=1)
    sinf = jnp.concatenate([-sin, sin, jnp.zeros((t, hd - rot), F32)], axis=1)
    return cosf, sinf


def _attn_kernel(q_ref, k_ref, v_ref, o_ref, ka_ref, vb_ref, km_ref, *, blk, nb, scale):
    qi = pl.program_id(2)
    s_len, hd = k_ref.shape

    @pl.when(qi == 0)
    def _():
        vb_ref[...] = v_ref[...].astype(BF16)
        ka_ref[:, :hd] = k_ref[...].astype(BF16)
        row = lax.broadcasted_iota(I32, (s_len, hd), 0)
        lane = lax.broadcasted_iota(I32, (s_len, hd), 1)
        member = (row >= lane * blk) & (row < (lane + 1) * blk)
        ka_ref[:, hd:] = jnp.where(member, 1.0, 0.0).astype(BF16)
        km_ref[...] = jnp.zeros_like(km_ref)
        for j in range(nb):
            km_ref[j:j + 1, :] = jnp.mean(k_ref[j * blk:(j + 1) * blk, :], axis=0, keepdims=True)

    q = q_ref[...]
    tq = q.shape[0]
    gate = lax.dot_general(q.astype(F32), km_ref[...], (((1,), (1,)), ((), ())),
                           precision=lax.Precision.HIGHEST, preferred_element_type=F32)
    lane = lax.broadcasted_iota(I32, (tq, hd), 1)
    g = jnp.where(lane < qi, gate, -jnp.inf)
    sel = lane == qi
    for _ in range(MOBA_TOPK):
        m = jnp.max(g, axis=1, keepdims=True)
        idx = jnp.min(jnp.where(g == m, lane, hd), axis=1, keepdims=True)
        hit = lane == idx
        sel = sel | (hit & (m > -jnp.inf))
        g = jnp.where(hit, -jnp.inf, g)
    bias = jnp.where(sel, 0.0, NEG_BIG).astype(BF16)
    qa = jnp.concatenate([q, bias], axis=1)
    nt = (((1,), (1,)), ((), ()))

    d0 = pl.multiple_of(qi * blk, blk)
    s = lax.dot_general(qa, ka_ref[pl.ds(d0, blk), :], nt, preferred_element_type=F32) * scale
    row = lax.broadcasted_iota(I32, (tq, blk), 0)
    col = lax.broadcasted_iota(I32, (tq, blk), 1)
    s = jnp.where(col <= row, s, -jnp.inf)
    m0 = jnp.max(s, axis=1, keepdims=True)
    p = jnp.exp(s - m0)
    l0 = jnp.sum(p, axis=1, keepdims=True)
    acc0 = jnp.dot(p.astype(BF16), vb_ref[pl.ds(d0, blk), :], preferred_element_type=F32)

    def body(j, carry):
        m, l, acc = carry
        r0 = pl.multiple_of(j * blk, blk)
        sj = lax.dot_general(qa, ka_ref[pl.ds(r0, blk), :], nt, preferred_element_type=F32) * scale
        m_new = jnp.maximum(m, jnp.max(sj, axis=1, keepdims=True))
        alpha = jnp.exp(m - m_new)
        pj = jnp.exp(sj - m_new)
        l = l * alpha + jnp.sum(pj, axis=1, keepdims=True)
        acc = acc * alpha + jnp.dot(pj.astype(BF16), vb_ref[pl.ds(r0, blk), :], preferred_element_type=F32)
        return m_new, l, acc

    _, l, acc = lax.fori_loop(0, qi, body, (m0, l0, acc0))
    o_ref[...] = (acc / l).astype(o_ref.dtype)


def _moba_prefill(q, k, v, n_seq, s_len, n_heads, hd):
    blk = MOBA_BLOCK
    nb = s_len // blk
    assert s_len % blk == 0 and nb <= hd
    t = n_seq * s_len
    return pl.pallas_call(
        functools.partial(_attn_kernel, blk=blk, nb=nb, scale=1.0 / math.sqrt(hd)),
        grid=(n_seq, n_heads, nb),
        in_specs=[pl.BlockSpec((blk, hd), lambda n, h, i: (n * nb + i, h)),
                  pl.BlockSpec((s_len, hd), lambda n, h, i: (n, h)),
                  pl.BlockSpec((s_len, hd), lambda n, h, i: (n, h))],
        out_specs=pl.BlockSpec((blk, hd), lambda n, h, i: (n * nb + i, h)),
        out_shape=jax.ShapeDtypeStruct((t, n_heads * hd), BF16),
        scratch_shapes=[pltpu.VMEM((s_len, 2 * hd), BF16), pltpu.VMEM((s_len, hd), BF16),
                        pltpu.VMEM((hd, hd), F32)],
        compiler_params=_cparams(3),
        name="moba_prefill",
    )(q, k, v)


def _dec_ksum_kernel(pt_ref, *refs, npg, ppb):
    o_ref = refs[npg]
    for r in range(npg // ppb):
        s = jnp.sum(refs[r * ppb][0], axis=0)
        for u in range(1, ppb):
            s = s + jnp.sum(refs[r * ppb + u][0], axis=0)
        o_ref[0, r] = s


def _dec_sel_kernel(ks_ref, q_ref, o_ref, *, blk):
    km = ks_ref[0] * (1.0 / blk)
    nbp, n_heads, _ = km.shape
    g = jnp.sum(km * q_ref[...], axis=2, keepdims=True)
    blk_id = lax.broadcasted_iota(I32, g.shape, 0)
    lane = lax.broadcasted_iota(I32, (n_heads, LANES), 1)
    out = jnp.full((n_heads, LANES), nbp, I32)
    for r in range(MOBA_TOPK):
        m = jnp.max(g, axis=0, keepdims=True)
        idx = jnp.min(jnp.where(g == m, blk_id, nbp), axis=0, keepdims=True)
        pick = jnp.where(m > -jnp.inf, idx, nbp)[0]
        out = jnp.where(lane == r, pick, out)
        g = jnp.where(blk_id == idx, -jnp.inf, g)
    o_ref[0] = out


def _dec_attn_kernel(pt_ref, sel_ref, q_ref, kn_ref, vn_ref, ck_ref, cv_ref, o_ref, kbuf, vbuf, sem,
                     *, nsel, ppb, nbp, scale):
    b = pl.program_id(0)
    nb = pl.num_programs(0)
    n_heads = q_ref.shape[1]
    npg = nsel * ppb

    def copies(bb, slot):
        out = []
        for h in range(n_heads):
            for r in range(nsel):
                blk_id = jnp.minimum(sel_ref[(bb * n_heads + h) * nsel + r], nbp - 1)
                for u in range(ppb):
                    phys = pt_ref[bb, blk_id * ppb + u]
                    j = h * npg + r * ppb + u
                    out.append(pltpu.make_async_copy(ck_ref.at[phys, :, h, :], kbuf.at[slot, j], sem.at[slot]))
                    out.append(pltpu.make_async_copy(cv_ref.at[phys, :, h, :], vbuf.at[slot, j], sem.at[slot]))
        return out

    slot = lax.rem(b, 2)

    @pl.when(b == 0)
    def _():
        for cp in copies(b, 0):
            cp.start()

    @pl.when(b + 1 < nb)
    def _():
        for cp in copies(b + 1, 1 - slot):
            cp.start()

    for cp in copies(b, slot):
        cp.wait()

    q8 = q_ref[0]
    page = kbuf.shape[2]
    for h in range(n_heads):
        ks = jnp.concatenate([kbuf[slot, h * npg + j] for j in range(npg)], axis=0)
        vs = jnp.concatenate([vbuf[slot, h * npg + j] for j in range(npg)], axis=0)
        s_all = lax.dot_general(q8.astype(BF16), ks.astype(BF16), (((1,), (1,)), ((), ())),
                                preferred_element_type=F32)
        s = s_all[h:h + 1] * scale
        col = lax.broadcasted_iota(I32, s.shape, 1)
        ok = jnp.zeros(s.shape, jnp.bool_)
        for r in range(nsel):
            in_seg = (col >= r * ppb * page) & (col < (r + 1) * ppb * page)
            ok = ok | (in_seg & (sel_ref[(b * n_heads + h) * nsel + r] < nbp))
        s = jnp.where(ok, s, -jnp.inf)
        qh, knh, vnh = q8[h:h + 1], kn_ref[0, h:h + 1], vn_ref[0, h:h + 1]
        s_new = jnp.sum(qh * knh, axis=1, keepdims=True) * scale
        m = jnp.maximum(jnp.max(s, axis=1, keepdims=True), s_new)
        p = jnp.exp(s - m)
        p_new = jnp.exp(s_new - m)
        den = jnp.sum(p, axis=1, keepdims=True) + p_new
        out = jnp.dot(p.astype(BF16), vs.astype(BF16), preferred_element_type=F32) + p_new * vnh
        o_ref[0, h:h + 1, :] = out / den


def _moba_decode(q, k_new, v_new, cache_k, cache_v, page_table):
    db, n_pages = page_table.shape
    _, page, n_heads, hd = cache_k.shape
    blk = MOBA_BLOCK
    ppb = blk // page
    nbp = n_pages // ppb
    assert blk % page == 0 and n_pages % ppb == 0 and nbp >= 1
    npg = 8 if n_pages % 8 == 0 else ppb
    nstep = n_pages // npg
    bps = npg // ppb

    def page_spec(r):
        return pl.BlockSpec((1, page, n_heads, hd), lambda b, s, pt: (pt[b, s * npg + r], 0, 0, 0))

    ksum = pl.pallas_call(
        functools.partial(_dec_ksum_kernel, npg=npg, ppb=ppb),
        grid_spec=pltpu.PrefetchScalarGridSpec(
            num_scalar_prefetch=1, grid=(db, nstep),
            in_specs=[page_spec(r) for r in range(npg)],
            out_specs=pl.BlockSpec((1, bps, n_heads, hd), lambda b, s, pt: (b, s, 0, 0))),
        out_shape=jax.ShapeDtypeStruct((db, nbp, n_heads, hd), F32),
        compiler_params=_cparams(2),
        name="dec_ksum",
    )(page_table, *([cache_k] * npg))

    sel = pl.pallas_call(
        functools.partial(_dec_sel_kernel, blk=blk),
        grid=(db,),
        in_specs=[pl.BlockSpec((1, nbp, n_heads, hd), lambda b: (b, 0, 0, 0)),
                  pl.BlockSpec((1, n_heads, hd), lambda b: (b, 0, 0))],
        out_specs=pl.BlockSpec((1, n_heads, LANES), lambda b: (b, 0, 0)),
        out_shape=jax.ShapeDtypeStruct((db, n_heads, LANES), I32),
        compiler_params=_cparams(1),
        name="dec_select",
    )(ksum, q)
    nsel = MOBA_TOPK
    sel = sel[:, :, :nsel].reshape(-1)

    row_spec = pl.BlockSpec((1, n_heads, hd), lambda b, pt, sl: (b, 0, 0))
    n_buf = n_heads * nsel * ppb
    return pl.pallas_call(
        functools.partial(_dec_attn_kernel, nsel=nsel, ppb=ppb, nbp=nbp, scale=1.0 / math.sqrt(hd)),
        grid_spec=pltpu.PrefetchScalarGridSpec(
            num_scalar_prefetch=2, grid=(db,),
            in_specs=[row_spec, row_spec, row_spec,
                      pl.BlockSpec(memory_space=pl.ANY), pl.BlockSpec(memory_space=pl.ANY)],
            out_specs=row_spec,
            scratch_shapes=[pltpu.VMEM((2, n_buf, page, hd), F32), pltpu.VMEM((2, n_buf, page, hd), F32),
                            pltpu.SemaphoreType.DMA((2,))]),
        out_shape=jax.ShapeDtypeStruct((db, n_heads, hd), F32),
        compiler_params=_cparams(1),
        name="dec_attend",
    )(page_table, sel, q, k_new, v_new, cache_k, cache_v)


def _s5_discretise(lam_re, lam_im, log_dt, b_re, b_im):
    dt = jnp.exp(log_dt.astype(F32))[:, None]
    mag = jnp.exp(lam_re * dt)
    ang = lam_im * dt
    a_re = mag * jnp.cos(ang)
    a_im = mag * jnp.sin(ang)
    den = lam_re * lam_re + lam_im * lam_im
    nr = a_re - 1.0
    f_re = (nr * lam_re + a_im * lam_im) / den
    f_im = (a_im * lam_re - nr * lam_im) / den
    bb_re = f_re[..., None] * b_re - f_im[..., None] * b_im
    bb_im = f_re[..., None] * b_im + f_im[..., None] * b_re
    return a_re, a_im, bb_re, bb_im


def _s5_layout(a_re, a_im, bb_re, bb_im, c_re, c_im, d_skip, n_rows):
    g, p, gc = bb_re.shape
    gb = max(1, min(g // 2, (2 * LANES) // gc, 16))
    nblk = g // gb
    eye = jnp.eye(gb, dtype=F32)

    def in_blocks(bb):
        return jnp.einsum('bgpc,gh->bgchp', bb.reshape(nblk, gb, p, gc), eye).reshape(nblk, gb * gc, gb * p)

    def out_blocks(c):
        return jnp.einsum('bgcp,gh->bgphc', c.reshape(nblk, gb, gc, p), eye).reshape(nblk, gb * p, gb * gc)

    bb = jnp.stack([in_blocks(bb_re), in_blocks(bb_im)]).astype(BF16)
    cc = jnp.concatenate([out_blocks(c_re), -out_blocks(c_im)], axis=1).astype(BF16)
    sh = g * p // 2
    a = jnp.stack([a_re.reshape(2, sh), a_im.reshape(2, sh)])
    a = jnp.repeat(a, n_rows // 2, axis=1)
    a = a.reshape(2, n_rows, sh // LANES, LANES).transpose(0, 2, 1, 3)
    return bb, cc, a, d_skip.reshape(1, g * gc).astype(F32), gb


def _s5_kernel(u_ref, bb_ref, cc_ref, a_ref, d_ref, z_ref, hre_ref, him_ref, xre, xim, hst,
               *, n_seq, length, pitch, nb_half, chains):
    c = pl.program_id(0)
    nrow = 2 * n_seq
    _, nblk, cb, sb = bb_ref.shape
    nsl = sb // LANES
    nslab = nb_half * nsl

    @pl.when(c == 0)
    def _():
        hst[...] = jnp.zeros_like(hst)

    u = u_ref[...].reshape(n_seq * length, u_ref.shape[2])
    ub = u.astype(BF16)
    for half in range(2):
        for b2 in range(nb_half):
            b = half * nb_half + b2
            lhs = ub[:, b * cb:(b + 1) * cb]
            for ri, xs in ((0, xre), (1, xim)):
                res = jnp.dot(lhs, bb_ref[ri, b], preferred_element_type=F32)
                for n in range(n_seq):
                    for s8 in range(nsl):
                        xs[b2 * nsl + s8, pl.ds((half * n_seq + n) * pitch, length), :] = (
                            res[n * length:(n + 1) * length, s8 * LANES:(s8 + 1) * LANES])

    for cg in range(nslab // chains):
        slabs = [cg * chains + kk for kk in range(chains)]
        ar = [a_ref[0, sl] for sl in slabs]
        ai = [a_ref[1, sl] for sl in slabs]
        init = (tuple(hst[0, sl] for sl in slabs), tuple(hst[1, sl] for sl in slabs))

        def step(t, carry, slabs=slabs, ar=ar, ai=ai):
            hr, hi = carry
            nr, ni = [], []
            for kk, sl in enumerate(slabs):
                rows = pl.ds(t, nrow, stride=pitch)
                xr = xre[sl, rows, :]
                xi = xim[sl, rows, :]
                r = ar[kk] * hr[kk] - ai[kk] * hi[kk] + xr
                i = ar[kk] * hi[kk] + ai[kk] * hr[kk] + xi
                xre[sl, rows, :] = r
                xim[sl, rows, :] = i
                nr.append(r)
                ni.append(i)
            return tuple(nr), tuple(ni)

        hr, hi = lax.fori_loop(0, length, step, init)
        for kk, sl in enumerate(slabs):
            hst[0, sl] = hr[kk]
            hst[1, sl] = hi[kk]

    for half in range(2):
        for b2 in range(nb_half):
            b = half * nb_half + b2

            def gather(xs):
                return jnp.concatenate(
                    [jnp.concatenate([xs[b2 * nsl + s8, pl.ds((half * n_seq + n) * pitch, length), :]
                                      for s8 in range(nsl)], axis=1) for n in range(n_seq)], axis=0)

            lhs = jnp.concatenate([gather(xre), gather(xim)], axis=1).astype(BF16)
            y = jnp.dot(lhs, cc_ref[b], preferred_element_type=F32)
            y = y + d_ref[:, b * cb:(b + 1) * cb] * u[:, b * cb:(b + 1) * cb]
            zz = _gelu_tanh(y).astype(z_ref.dtype)
            for n in range(n_seq):
                z_ref[n, :, b * cb:(b + 1) * cb] = zz[n * length:(n + 1) * length]

    @pl.when(c == pl.num_programs(0) - 1)
    def _():
        for sl in range(nslab):
            hre_ref[:, sl * LANES:(sl + 1) * LANES] = hst[0, sl]
            him_ref[:, sl * LANES:(sl + 1) * LANES] = hst[1, sl]


def _s5_prompt(u, s5p, n_seq, s_len):
    a_re, a_im, bb_re, bb_im, c_re, c_im, d_skip = s5p
    g, p, gc = bb_re.shape
    nrow = 2 * n_seq
    assert nrow == SUBLANES, "the scan packs (half, sequence) on the eight sublanes"
    bb, cc, a, d, gb = _s5_layout(a_re, a_im, bb_re, bb_im, c_re, c_im, d_skip, nrow)
    nblk = g // gb
    nb_half = nblk // 2
    sb = gb * p
    nslab = nb_half * sb // LANES
    sh = g * p // 2
    length = min(128, s_len)
    pitch = length + SUBLANES
    chains = 4 if nslab % 4 == 0 else 1
    d_ssm = g * gc
    z, hre, him = pl.pallas_call(
        functools.partial(_s5_kernel, n_seq=n_seq, length=length, pitch=pitch, nb_half=nb_half, chains=chains),
        grid=(s_len // length,),
        in_specs=[pl.BlockSpec((n_seq, length, d_ssm), lambda c: (0, c, 0)),
                  pl.BlockSpec(bb.shape, lambda c: (0, 0, 0, 0)),
                  pl.BlockSpec(cc.shape, lambda c: (0, 0, 0)),
                  pl.BlockSpec(a.shape, lambda c: (0, 0, 0, 0)),
                  pl.BlockSpec(d.shape, lambda c: (0, 0))],
        out_specs=[pl.BlockSpec((n_seq, length, d_ssm), lambda c: (0, c, 0)),
                   pl.BlockSpec((nrow, sh), lambda c: (0, 0)),
                   pl.BlockSpec((nrow, sh), lambda c: (0, 0))],
        out_shape=[jax.ShapeDtypeStruct((n_seq, s_len, d_ssm), BF16),
                   jax.ShapeDtypeStruct((nrow, sh), F32),
                   jax.ShapeDtypeStruct((nrow, sh), F32)],
        scratch_shapes=[pltpu.VMEM((nslab, nrow * pitch, LANES), F32),
                        pltpu.VMEM((nslab, nrow * pitch, LANES), F32),
                        pltpu.VMEM((2, nslab, nrow, LANES), F32)],
        compiler_params=_cparams(1),
        name="s5_prompt",
    )(u, bb, cc, a, d)

    def unpack(h):
        return h.reshape(2, n_seq, g // 2, p).transpose(1, 0, 2, 3).reshape(n_seq, g, p)

    return z, unpack(hre), unpack(him)


def _s5_step_kernel(u_ref, h0r_ref, h0i_ref, bb_ref, cc_ref, ar_ref, ai_ref, d_ref, z_ref, hr_ref, hi_ref):
    _, nblk, cb, sb = bb_ref.shape
    u = u_ref[...]
    ub = u.astype(BF16)
    for b in range(nblk):
        lhs = ub[:, b * cb:(b + 1) * cb]
        st = slice(b * sb, (b + 1) * sb)
        x_re = jnp.dot(lhs, bb_ref[0, b], preferred_element_type=F32)
        x_im = jnp.dot(lhs, bb_ref[1, b], preferred_element_type=F32)
        a_re, a_im = ar_ref[:, st], ai_ref[:, st]
        h0r, h0i = h0r_ref[:, st], h0i_ref[:, st]
        h_re = a_re * h0r - a_im * h0i + x_re
        h_im = a_re * h0i + a_im * h0r + x_im
        hr_ref[:, st] = h_re
        hi_ref[:, st] = h_im
        lhs2 = jnp.concatenate([h_re, h_im], axis=1).astype(BF16)
        y = jnp.dot(lhs2, cc_ref[b], preferred_element_type=F32)
        y = y + d_ref[:, b * cb:(b + 1) * cb] * u[:, b * cb:(b + 1) * cb]
        z_ref[:, b * cb:(b + 1) * cb] = _gelu_tanh(y).astype(z_ref.dtype)


def _s5_sample(u, h0_re, h0_im, s5p):
    a_re, a_im, bb_re, bb_im, c_re, c_im, d_skip = s5p
    g, p, gc = bb_re.shape
    db = u.shape[0]
    bb, cc, _, d, _ = _s5_layout(a_re, a_im, bb_re, bb_im, c_re, c_im, d_skip, SUBLANES)
    ns = g * p
    z, hr, hi = pl.pallas_call(
        _s5_step_kernel,
        out_shape=[jax.ShapeDtypeStruct((db, g * gc), BF16),
                   jax.ShapeDtypeStruct((db, ns), F32),
                   jax.ShapeDtypeStruct((db, ns), F32)],
        compiler_params=pltpu.CompilerParams(vmem_limit_bytes=VMEM_LIMIT_BYTES),
        name="s5_sample",
    )(u, h0_re.reshape(db, ns), h0_im.reshape(db, ns), bb, cc,
      a_re.reshape(1, ns), a_im.reshape(1, ns), d)
    return z, hr.reshape(db, g, p), hi.reshape(db, g, p)


def _mix_kernel(z_ref, at_ref, ga_ref, gb_ref, x_ref, wv_ref, wg_ref, wup_ref, wo_ref, g1_ref, b1_ref,
                wr_ref, br_ref, h_ref, route_ref, *, alpha, n_exp):
    z = z_ref[...]
    br_ssm = (jnp.dot(z, wv_ref[...], preferred_element_type=F32)
              * _sigmoid(jnp.dot(z, wg_ref[...], preferred_element_type=F32)))
    br_attn = jnp.dot(at_ref[...].astype(BF16), wup_ref[...], preferred_element_type=F32)
    mix = _sigmoid(ga_ref[...]) * br_ssm + _sigmoid(gb_ref[...]) * br_attn
    r = alpha * x_ref[...] + jnp.dot(mix.astype(BF16), wo_ref[...], preferred_element_type=F32)
    h = _layernorm(r, g1_ref[...], b1_ref[...])
    h_ref[...] = h
    logits = jnp.dot(h.astype(BF16), wr_ref[...], preferred_element_type=F32) + br_ref[...]
    lane = lax.broadcasted_iota(I32, logits.shape, 1)
    lg = jnp.where(lane < n_exp, logits, -jnp.inf)
    vals, ids = [], []
    for _ in range(TOP_K):
        m = jnp.max(lg, axis=1, keepdims=True)
        idx = jnp.min(jnp.where(lg == m, lane, LANES), axis=1, keepdims=True)
        vals.append(m)
        ids.append(idx)
        lg = jnp.where(lane == idx, -jnp.inf, lg)
    ex = [jnp.exp(v - vals[0]) for v in vals]
    den = ex[0]
    for e in ex[1:]:
        den = den + e
    route = jnp.zeros(logits.shape, F32)
    for kk in range(TOP_K):
        route = jnp.where(lane == kk, ids[kk].astype(F32), route)
        route = jnp.where(lane == TOP_K + kk, ex[kk] / den, route)
    route_ref[...] = route


def _mix(z, attn, ga, gb, x, wts, alpha, n_exp, tm):
    t, d = x.shape
    tm = min(tm, t)
    wv, wg, wup, wo, g1, b1, wr, br = wts

    def rows(w):
        return pl.BlockSpec((tm, w), lambda i: (i, 0))

    def whole(a):
        return pl.BlockSpec(a.shape, lambda i: (0,) * a.ndim, pipeline_mode=pl.Buffered(1))

    return pl.pallas_call(
        functools.partial(_mix_kernel, alpha=alpha, n_exp=n_exp),
        grid=(t // tm,),
        in_specs=[rows(z.shape[1]), rows(attn.shape[1]), rows(d), rows(d), rows(d)] + [whole(w) for w in wts],
        out_specs=[rows(d), rows(LANES)],
        out_shape=[jax.ShapeDtypeStruct((t, d), F32), jax.ShapeDtypeStruct((t, LANES), F32)],
        compiler_params=_cparams(1),
        name="mix_ln1_router",
    )(z, attn, ga, gb, x, *wts)


def _rank_kernel(r_ref, rank_ref, cnt_ref, carry):
    @pl.when(pl.program_id(0) == 0)
    def _():
        carry[...] = jnp.zeros_like(carry)

    r = r_ref[...]
    tm = r.shape[0]
    lane = lax.broadcasted_iota(I32, r.shape, 1)
    ids = [r[:, kk:kk + 1].astype(I32) for kk in range(TOP_K)]
    onehot = jnp.zeros(r.shape, F32)
    for e in ids:
        onehot = onehot + (lane == e).astype(F32)
    tri = (lax.broadcasted_iota(I32, (tm, tm), 0) > lax.broadcasted_iota(I32, (tm, tm), 1)).astype(BF16)
    before = jnp.dot(tri, onehot.astype(BF16), preferred_element_type=F32) + carry[...]
    out = jnp.zeros(r.shape, F32)
    for kk, e in enumerate(ids):
        rk = jnp.sum(jnp.where(lane == e, before, 0.0), axis=1, keepdims=True)
        out = jnp.where(lane == kk, rk, out)
    rank_ref[...] = out.astype(I32)
    carry[...] = carry[...] + jnp.sum(onehot, axis=0, keepdims=True)
    cnt_ref[...] = carry[...]


def _ranks(route_all, tm):
    t = route_all.shape[0]
    return pl.pallas_call(
        _rank_kernel,
        grid=(t // tm,),
        in_specs=[pl.BlockSpec((tm, LANES), lambda i: (i, 0))],
        out_specs=[pl.BlockSpec((tm, LANES), lambda i: (i, 0)), pl.BlockSpec((1, LANES), lambda i: (0, 0))],
        out_shape=[jax.ShapeDtypeStruct((t, LANES), I32), jax.ShapeDtypeStruct((1, LANES), F32)],
        scratch_shapes=[pltpu.VMEM((1, LANES), F32)],
        compiler_params=_cparams(1),
        name="moe_rank",
    )(route_all)


def _dispatch_kernel(dest_ref, h_ref, xs_in, xs_out, sem):
    del xs_in
    tm = h_ref.shape[0]

    def row_copy(r, d):
        return pltpu.make_async_copy(h_ref.at[pl.ds(r, 1), :], xs_out.at[pl.ds(d, 1), :], sem)

    def issue(r, _):
        for kk in range(TOP_K):
            row_copy(r, dest_ref[r * TOP_K + kk]).start()
        return 0

    lax.fori_loop(0, tm, issue, 0)

    def drain(r, _):
        for kk in range(TOP_K):
            row_copy(r, dest_ref[r * TOP_K + kk]).wait()
        return 0

    lax.fori_loop(0, tm, drain, 0)


def _dispatch(h, dest, xs, tm):
    t, d = h.shape
    tm = min(tm, t)
    return pl.pallas_call(
        _dispatch_kernel,
        grid=(t // tm,),
        in_specs=[pl.BlockSpec((tm * TOP_K,), lambda i: (i,), memory_space=pltpu.SMEM),
                  pl.BlockSpec((tm, d), lambda i: (i, 0)),
                  pl.BlockSpec(memory_space=pl.ANY)],
        out_specs=pl.BlockSpec(memory_space=pl.ANY),
        out_shape=jax.ShapeDtypeStruct(xs.shape, xs.dtype),
        scratch_shapes=[pltpu.SemaphoreType.DMA(())],
        input_output_aliases={2: 0},
        compiler_params=_cparams(1),
        name="moe_dispatch",
    )(dest, h, xs)


def _expert_changed(be_ref, i, ic):
    prev = be_ref[jnp.maximum(ic - 1, 0)]
    return (i == 0) | (be_ref[ic] != prev)


def _g1_kernel(be_ref, nu_ref, xs_ref, wg_ref, wu_ref, bg_ref, bu_ref, act_ref, wgb, wub):
    i = pl.program_id(1)
    nu = nu_ref[0]

    @pl.when(i < nu)
    def _():
        @pl.when(_expert_changed(be_ref, i, i))
        def _():
            wgb[...] = wg_ref[0].astype(BF16)
            wub[...] = wu_ref[0].astype(BF16)

        x = xs_ref[...].astype(BF16)
        g = jnp.dot(x, wgb[...], preferred_element_type=F32) + bg_ref[0]
        up = jnp.dot(x, wub[...], preferred_element_type=F32) + bu_ref[0]
        g = jnp.minimum(g, SWIGLU_LIMIT)
        up = jnp.clip(up, -SWIGLU_LIMIT, SWIGLU_LIMIT)
        act_ref[...] = ((up + 1.0) * (g * _sigmoid(SWIGLU_ALPHA * g))).astype(act_ref.dtype)

    @pl.when(i >= nu)
    def _():
        act_ref[...] = jnp.zeros_like(act_ref)


def _g2_kernel(be_ref, nu_ref, act_ref, wd_ref, bd_ref, y_ref, wdb):
    i = pl.program_id(0)
    nu = nu_ref[0]

    @pl.when(i < nu)
    def _():
        @pl.when(_expert_changed(be_ref, i, i))
        def _():
            wdb[...] = wd_ref[0].astype(BF16)

        y_ref[...] = jnp.dot(act_ref[...], wdb[...], preferred_element_type=F32) + bd_ref[0]

    @pl.when(i >= nu)
    def _():
        y_ref[...] = jnp.zeros_like(y_ref)


def _experts(xs, blk_e, n_used, w_gate, b_gate, w_up, b_up, w_down, b_down):
    n_rows, d = xs.shape
    n_exp, _, dff = w_gate.shape
    nblk = n_rows // MOE_BLOCK
    tf = min(1024, dff)
    nf = dff // tf

    def blk_row(f, i, be, nu):
        return (jnp.minimum(i, nu[0] - 1), 0)

    def w_spec(f, i, be, nu):
        return (be[jnp.minimum(i, nu[0] - 1)], 0, f)

    act = pl.pallas_call(
        _g1_kernel,
        grid_spec=pltpu.PrefetchScalarGridSpec(
            num_scalar_prefetch=2, grid=(nf, nblk),
            in_specs=[pl.BlockSpec((MOE_BLOCK, d), blk_row),
                      pl.BlockSpec((1, d, tf), w_spec), pl.BlockSpec((1, d, tf), w_spec),
                      pl.BlockSpec((1, 1, tf), w_spec), pl.BlockSpec((1, 1, tf), w_spec)],
            out_specs=pl.BlockSpec((MOE_BLOCK, tf), lambda f, i, be, nu: (i, f)),
            scratch_shapes=[pltpu.VMEM((d, tf), BF16), pltpu.VMEM((d, tf), BF16)]),
        out_shape=jax.ShapeDtypeStruct((n_rows, dff), BF16),
        compiler_params=_cparams(2),
        name="moe_gate_up",
    )(blk_e, n_used, xs, w_gate, w_up, b_gate.reshape(n_exp, 1, dff), b_up.reshape(n_exp, 1, dff))

    def w2_spec(i, be, nu):
        return (be[jnp.minimum(i, nu[0] - 1)], 0, 0)

    return pl.pallas_call(
        _g2_kernel,
        grid_spec=pltpu.PrefetchScalarGridSpec(
            num_scalar_prefetch=2, grid=(nblk,),
            in_specs=[pl.BlockSpec((MOE_BLOCK, dff), lambda i, be, nu: (jnp.minimum(i, nu[0] - 1), 0)),
                      pl.BlockSpec((1, dff, d), w2_spec), pl.BlockSpec((1, 1, d), w2_spec)],
            out_specs=pl.BlockSpec((MOE_BLOCK, d), lambda i, be, nu: (i, 0)),
            scratch_shapes=[pltpu.VMEM((dff, d), BF16)]),
        out_shape=jax.ShapeDtypeStruct((n_rows, d), F32),
        compiler_params=_cparams(1),
        name="moe_down",
    )(blk_e, n_used, act, w_down, b_down.reshape(n_exp, 1, d))


def _combine_kernel(dest_ref, h_ref, route_ref, g_ref, b_ref, yb_ref, o_ref, buf, sem, *, alpha):
    tm = h_ref.shape[0]

    def row_copy(r, kk, d):
        return pltpu.make_async_copy(yb_ref.at[pl.ds(d, 1), :], buf.at[kk, pl.ds(r, 1), :], sem)

    def issue(r, _):
        for kk in range(TOP_K):
            row_copy(r, kk, dest_ref[r * TOP_K + kk]).start()
        return 0

    lax.fori_loop(0, tm, issue, 0)

    def drain(r, _):
        for kk in range(TOP_K):
            row_copy(r, kk, dest_ref[r * TOP_K + kk]).wait()
        return 0

    lax.fori_loop(0, tm, drain, 0)
    route = route_ref[...]
    acc = alpha * h_ref[...]
    for kk in range(TOP_K):
        acc = acc + route[:, TOP_K + kk:TOP_K + kk + 1] * buf[kk]
    o_ref[...] = _layernorm(acc, g_ref[...], b_ref[...])


def _combine(h, route, dest, yb, g2, b2, alpha, tm):
    t, d = h.shape
    tm = min(tm, t)
    return pl.pallas_call(
        functools.partial(_combine_kernel, alpha=alpha),
        grid=(t // tm,),
        in_specs=[pl.BlockSpec((tm * TOP_K,), lambda i: (i,), memory_space=pltpu.SMEM),
                  pl.BlockSpec((tm, d), lambda i: (i, 0)),
                  pl.BlockSpec((tm, LANES), lambda i: (i, 0)),
                  pl.BlockSpec((1, d), lambda i: (0, 0)),
                  pl.BlockSpec((1, d), lambda i: (0, 0)),
                  pl.BlockSpec(memory_space=pl.ANY)],
        out_specs=pl.BlockSpec((tm, d), lambda i: (i, 0)),
        out_shape=jax.ShapeDtypeStruct((t, d), F32),
        scratch_shapes=[pltpu.VMEM((TOP_K, tm, d), F32), pltpu.SemaphoreType.DMA(())],
        compiler_params=_cparams(1),
        name="moe_combine_ln2",
    )(dest, h, route, g2, b2, yb)


def _layer(x_prompt, x_sample, cache_k, cache_v, page_table, st_re, st_im, w_in,
           lam_re, lam_im, log_dt, b_re, b_im, c_re, c_im, d_skip,
           w_glu_val, w_glu_gate, w_attn_up, w_o, ln1_g, ln1_b,
           w_router, b_router, w_gate, b_gate, w_up, b_up, w_down, b_down, ln2_g, ln2_b, depth):
    n_seq, s_len, d_model = x_prompt.shape
    db, dec_seq, _ = x_sample.shape
    assert dec_seq == 1, "decode path handles one new token per sequence"
    n_phys, page, n_heads, hd = cache_k.shape
    d_attn = n_heads * hd
    n_groups, n_state = st_re.shape[1:]
    d_ssm = n_groups * b_re.shape[-1]
    n_exp = w_router.shape[1]
    alpha = (2 * depth) ** 0.25
    tp, ts = n_seq * s_len, db
    past = page_table.shape[1] * page
    col_q, col_k, col_v, col_u = 0, d_attn, 2 * d_attn, 3 * d_attn
    col_ga, col_gb = 3 * d_attn + d_ssm, 3 * d_attn + d_ssm + d_model

    wb = w_in.astype(BF16)
    s5p = _s5_discretise(lam_re.astype(F32), lam_im.astype(F32), log_dt, b_re.astype(F32), b_im.astype(F32))
    s5p = s5p + (c_re.astype(F32), c_im.astype(F32), d_skip)
    wr_pad = jnp.zeros((d_model, LANES), BF16).at[:, :n_exp].set(w_router.astype(BF16))
    br_pad = jnp.zeros((1, LANES), F32).at[0, :n_exp].set(b_router.astype(F32))
    mix_w = (w_glu_val.astype(BF16), w_glu_gate.astype(BF16), w_attn_up.astype(BF16), w_o.astype(BF16),
             ln1_g.reshape(1, d_model).astype(F32), ln1_b.reshape(1, d_model).astype(F32), wr_pad, br_pad)

    def project(x2, pos, tm, tag):
        xb = x2.astype(BF16)
        tabs = _rope_tables(pos, hd)
        kw = dict(tm=tm, tn=1024)
        q = _proj(xb, wb, col_q, d_attn, BF16 if tag == "p" else F32, rope_tabs=tabs, hd=hd, name=f"proj_q_{tag}", **kw)
        k = _proj(xb, wb, col_k, d_attn, F32, rope_tabs=tabs, hd=hd, name=f"proj_k_{tag}", **kw)
        v = _proj(xb, wb, col_v, d_attn, F32, name=f"proj_v_{tag}", **kw)
        u = _proj(xb, wb, col_u, d_ssm, F32, name=f"proj_u_{tag}", **kw)
        ga = _proj(xb, wb, col_ga, d_model, F32, name=f"proj_ga_{tag}", **kw)
        gb = _proj(xb, wb, col_gb, d_model, F32, name=f"proj_gb_{tag}", **kw)
        return q, k, v, u, ga, gb

    xp2 = x_prompt.reshape(tp, d_model)
    q1, k1, v1, u1, ga1, gb1 = project(xp2, jnp.tile(jnp.arange(s_len), n_seq), 512, "p")
    attn1 = _moba_prefill(q1, k1, v1, n_seq, s_len, n_heads, hd)
    z1, hr1, hi1 = _s5_prompt(u1.reshape(n_seq, s_len, d_ssm), s5p, n_seq, s_len)
    h1, route1 = _mix(z1.reshape(tp, d_ssm), attn1, ga1, gb1, xp2, mix_w, alpha, n_exp, 256)

    xs2 = x_sample.reshape(ts, d_model)
    q2, k2, v2, u2, ga2, gb2 = project(xs2, jnp.full((ts,), past, I32), ts, "s")
    attn2 = _moba_decode(q2.reshape(ts, n_heads, hd), k2.reshape(ts, n_heads, hd), v2.reshape(ts, n_heads, hd),
                         cache_k, cache_v, page_table)
    z2, hr2, hi2 = _s5_sample(u2, st_re, st_im, s5p)
    h2, route2 = _mix(z2, attn2.reshape(ts, d_attn), ga2, gb2, xs2, mix_w, alpha, n_exp, ts)

    tr = 256
    t_all = tp + ts
    t_pad = -(-t_all // tr) * tr
    route_all = jnp.concatenate([route1, route2, jnp.full((t_pad - t_all, LANES), -1.0, F32)], axis=0)
    rank_all, cnt = _ranks(route_all, tr)
    counts = cnt[0, :n_exp].astype(I32)
    padded = (counts + MOE_BLOCK - 1) // MOE_BLOCK * MOE_BLOCK
    pad_end = jnp.cumsum(padded)
    pad_start = pad_end - padded
    ids = route_all[:t_all, :TOP_K].astype(I32)
    dest = (pad_start[ids] + rank_all[:t_all, :TOP_K]).astype(I32)
    nblk = -(-(t_all * TOP_K) // MOE_BLOCK) + n_exp
    blk_e = jnp.minimum(jnp.searchsorted(pad_end, jnp.arange(nblk) * MOE_BLOCK, side='right'), n_exp - 1).astype(I32)
    n_used = (pad_end[-1:] // MOE_BLOCK).astype(I32)
    dest1 = dest[:tp].reshape(-1)
    dest2 = dest[tp:].reshape(-1)

    xs = jnp.zeros((nblk * MOE_BLOCK, d_model), F32)
    xs = _dispatch(h1, dest1, xs, 256)
    xs = _dispatch(h2, dest2, xs, ts)
    yb = _experts(xs, blk_e, n_used, w_gate, b_gate, w_up, b_up, w_down, b_down)
    g2 = ln2_g.reshape(1, d_model).astype(F32)
    b2 = ln2_b.reshape(1, d_model).astype(F32)
    y1 = _combine(h1, route1, dest1, yb, g2, b2, alpha, 256)
    y2 = _combine(h2, route2, dest2, yb, g2, b2, alpha, ts)

    return (y1.reshape(n_seq, s_len, d_model), y2.reshape(db, 1, d_model),
            k1.reshape(n_seq, s_len, n_heads, hd), v1.reshape(n_seq, s_len, n_heads, hd), hr1, hi1,
            k2.reshape(db, 1, n_heads, hd), v2.reshape(db, 1, n_heads, hd), hr2, hi2)


def kernel(x_prompt, x_sample, cache_k, cache_v, page_table, state_ssm_re, state_ssm_im, w_in, ssm_lambda_re, ssm_lambda_im, ssm_log_dt, ssm_b_re, ssm_b_im, ssm_c_re, ssm_c_im, ssm_d, w_glu_val, w_glu_gate, w_attn_up, w_o, ln1_g, ln1_b, w_router, b_router, w_gate, b_gate, w_up, b_up, w_down, b_down, ln2_g, ln2_b):
    depth = w_in.shape[0]
    assert depth == 1, "single-layer step"
    outs = _layer(x_prompt, x_sample, cache_k[0], cache_v[0], page_table, state_ssm_re[0], state_ssm_im[0],
                  w_in[0], ssm_lambda_re[0], ssm_lambda_im[0], ssm_log_dt[0], ssm_b_re[0], ssm_b_im[0],
                  ssm_c_re[0], ssm_c_im[0], ssm_d[0], w_glu_val[0], w_glu_gate[0], w_attn_up[0], w_o[0],
                  ln1_g[0], ln1_b[0], w_router[0], b_router[0], w_gate[0], b_gate[0], w_up[0], b_up[0],
                  w_down[0], b_down[0], ln2_g[0], ln2_b[0], depth)
    y1, y2, k1, v1, hr1, hi1, k2, v2, hr2, hi2 = outs
    return (y1, y2, k1[None], v1[None], hr1[None], hi1[None], k2[None], v2[None], hr2[None], hi2[None])
```

```python
import functools
import math

import jax
import jax.numpy as jnp
from jax import lax
from jax.experimental import pallas as pl
from jax.experimental.pallas import tpu as pltpu

F32, BF16, I32 = jnp.float32, jnp.bfloat16, jnp.int32

ROPE_THETA = 500000.0
ROT_FRACTION = 4
MOBA_BLOCK = 256
MOBA_TOPK = 3
TOP_K = 4
SWIGLU_LIMIT = 7.0
SWIGLU_ALPHA = 1.702
LN_EPS = 1e-5
MOE_BLOCK = 256

LANES = 128
SUBLANES = 8
VMEM_LIMIT_BYTES = 56 * 1024 * 1024

NEG_BIG = -1e30


def _cparams(n_axes):
    return pltpu.CompilerParams(dimension_semantics=("arbitrary",) * n_axes,
                                vmem_limit_bytes=VMEM_LIMIT_BYTES)


def _sigmoid(x):
    return 1.0 / (1.0 + jnp.exp(-x))


def _gelu_tanh(x):
    c = math.sqrt(2.0 / math.pi)
    return x * (0.5 * (1.0 + jnp.tanh(c * (x + 0.044715 * (x * x * x)))))


def _layernorm(r, g, b):
    mu = jnp.mean(r, axis=-1, keepdims=True)
    xc = r - mu
    var = jnp.mean(xc * xc, axis=-1, keepdims=True)
    return xc * lax.rsqrt(var + LN_EPS) * g + b


def _proj_kernel(x_ref, w_ref, *rest, rope, rot_half, hd):
    acc = jnp.dot(x_ref[...], w_ref[...], preferred_element_type=F32)
    if not rope:
        (o_ref,) = rest
        o_ref[...] = acc.astype(o_ref.dtype)
        return
    cos_ref, sin_ref, o_ref = rest
    cos = cos_ref[...]
    sin = sin_ref[...]
    first = lax.broadcasted_iota(I32, cos.shape, 1) < rot_half
    for hh in range(acc.shape[1] // hd):
        xh = acc[:, hh * hd:(hh + 1) * hd]
        rot = jnp.where(first, pltpu.roll(xh, hd - rot_half, 1), pltpu.roll(xh, rot_half, 1))
        o_ref[:, hh * hd:(hh + 1) * hd] = (xh * cos + rot * sin).astype(o_ref.dtype)


def _proj(xb, wb, col0, ncols, out_dtype, *, tm, tn, rope_tabs=None, hd=LANES, name):
    t, k = xb.shape
    tm = min(tm, t)
    tn = min(tn, ncols)
    off = col0 // tn
    assert col0 % tn == 0 and ncols % tn == 0 and t % tm == 0
    in_specs = [pl.BlockSpec((tm, k), lambda i, j: (i, 0)),
                pl.BlockSpec((k, tn), lambda i, j: (0, j + off))]
    args = [xb, wb]
    if rope_tabs is not None:
        in_specs += [pl.BlockSpec((tm, hd), lambda i, j: (i, 0))] * 2
        args += list(rope_tabs)
    return pl.pallas_call(
        functools.partial(_proj_kernel, rope=rope_tabs is not None, rot_half=hd // ROT_FRACTION // 2, hd=hd),
        grid=(t // tm, ncols // tn),
        in_specs=in_specs,
        out_specs=pl.BlockSpec((tm, tn), lambda i, j: (i, j)),
        out_shape=jax.ShapeDtypeStruct((t, ncols), out_dtype),
        compiler_params=_cparams(2),
        name=name,
    )(*args)


def _rope_tables(pos, hd):
    rot = hd // ROT_FRACTION
    half = rot // 2
    inv = ROPE_THETA ** (-jnp.arange(half, dtype=F32) * 2.0 / rot)
    ang = pos.astype(F32)[:, None] * inv[None, :]
    cos, sin = jnp.cos(ang), jnp.sin(ang)
    t = pos.shape[0]
    cosf = jnp.concatenate([cos, cos, jnp.ones((t, hd - rot), F32)], axis=1)
    sinf = jnp.concatenate([-sin, sin, jnp.zeros((t, hd - rot), F32)], axis=1)
    return cosf, sinf


def _page_sums(page_refs, o_ref, ppb):
    for r in range(len(page_refs) // ppb):
        s = jnp.sum(page_refs[r * ppb][0], axis=0)
        for u in range(1, ppb):
            s = s + jnp.sum(page_refs[r * ppb + u][0], axis=0)
        o_ref[0, r] = s


def _attn_kernel(pt_ref, q_ref, k_ref, v_ref, *rest, blk, nb, grp, scale, npg, ppb, ksteps):
    del pt_ref
    page_refs = rest[:npg]
    if npg:
        o_ref, ks_ref, ka_ref, vb_ref, km_ref, s_ref = rest[npg:]
        step = (pl.program_id(0) * pl.num_programs(1) + pl.program_id(1)) * pl.num_programs(2) + pl.program_id(2)

        @pl.when(step < ksteps)
        def _():
            _page_sums(page_refs, ks_ref, ppb)
    else:
        o_ref, ka_ref, vb_ref, km_ref, s_ref = rest

    qi = pl.program_id(2)
    s_len, hd = k_ref.shape

    @pl.when(qi == 0)
    def _():
        vb_ref[...] = v_ref[...].astype(BF16)
        ka_ref[:, :hd] = k_ref[...].astype(BF16)
        row = lax.broadcasted_iota(I32, (s_len, hd), 0)
        lane = lax.broadcasted_iota(I32, (s_len, hd), 1)
        member = (row >= lane * blk) & (row < (lane + 1) * blk)
        ka_ref[:, hd:] = jnp.where(member, 1.0, 0.0).astype(BF16)
        km_ref[...] = jnp.zeros_like(km_ref)
        for j in range(nb):
            km_ref[j:j + 1, :] = jnp.mean(k_ref[j * blk:(j + 1) * blk, :], axis=0, keepdims=True)

    q = q_ref[...]
    tq = q.shape[0]
    nt = (((1,), (1,)), ((), ()))
    nbp = -(-nb // SUBLANES) * SUBLANES
    gate_t = lax.dot_general(km_ref[:nbp, :], q.astype(F32), nt,
                             precision=lax.Precision.HIGHEST, preferred_element_type=F32)
    blk_id = lax.broadcasted_iota(I32, (nbp, tq), 0)
    g = jnp.where(blk_id < qi, gate_t, -jnp.inf)
    sel = jnp.zeros(g.shape, jnp.bool_)
    for _ in range(MOBA_TOPK):
        m = jnp.max(g, axis=0, keepdims=True)
        idx = jnp.min(jnp.where(g == m, blk_id, nbp), axis=0, keepdims=True)
        hit = blk_id == idx
        sel = sel | (hit & (m > -jnp.inf))
        g = jnp.where(hit, -jnp.inf, g)
    bias_t = jnp.concatenate([jnp.where(sel, 0.0, NEG_BIG), jnp.full((hd - nbp, tq), NEG_BIG, F32)], axis=0)
    qa = jnp.concatenate([q, bias_t.T.astype(BF16)], axis=1)
    gw = grp * blk
    ntile = blk // LANES

    def lane_fold(x, acc, op):
        for c in range(x.shape[1] // LANES):
            acc = op(acc, x[:, c * LANES:(c + 1) * LANES])
        return acc

    d0 = pl.multiple_of(qi * blk, blk)
    s_own = lax.dot_general(q, ka_ref[pl.ds(d0, blk), :hd], nt, preferred_element_type=F32) * scale
    row = lax.broadcasted_iota(I32, (tq, blk), 0)
    col = lax.broadcasted_iota(I32, (tq, blk), 1)
    s_own = jnp.where(col <= row, s_own, -jnp.inf)
    m_part = lane_fold(s_own[:, LANES:], s_own[:, :LANES], jnp.maximum) if ntile > 1 else s_own

    ng = lax.div(qi + (grp - 1), grp)

    def pass1(gi, m_part):
        r0 = pl.multiple_of(gi * gw, gw)
        s = lax.dot_general(qa, ka_ref[pl.ds(r0, gw), :], nt, preferred_element_type=F32) * scale
        s_ref[gi] = s
        return lane_fold(s, m_part, jnp.maximum)

    m_part = lax.fori_loop(0, ng, pass1, m_part)
    m = jnp.max(m_part, axis=1, keepdims=True)

    p_own = jnp.exp(s_own - m)
    l_part = lane_fold(p_own[:, LANES:], p_own[:, :LANES], jnp.add) if ntile > 1 else p_own
    acc = jnp.dot(p_own.astype(BF16), vb_ref[pl.ds(d0, blk), :], preferred_element_type=F32)

    def pass2(gi, carry):
        l_part, acc = carry
        r0 = pl.multiple_of(gi * gw, gw)
        p = jnp.exp(s_ref[gi] - m)
        acc = acc + jnp.dot(p.astype(BF16), vb_ref[pl.ds(r0, gw), :], preferred_element_type=F32)
        return lane_fold(p, l_part, jnp.add), acc

    l_part, acc = lax.fori_loop(0, ng, pass2, (l_part, acc))
    o_ref[...] = (acc / jnp.sum(l_part, axis=1, keepdims=True)).astype(o_ref.dtype)


def _ksum_plan(page_table, cache_k):
    db, n_pages = page_table.shape
    page = cache_k.shape[1]
    ppb = MOBA_BLOCK // page
    assert MOBA_BLOCK % page == 0 and n_pages % ppb == 0 and n_pages >= ppb
    npg = 8 if n_pages % 8 == 0 else ppb
    return db, n_pages // ppb, ppb, npg, n_pages // npg


def _moba_prefill(q, k, v, n_seq, s_len, n_heads, hd, page_table, cache_k):
    blk = MOBA_BLOCK
    nb = s_len // blk
    assert s_len % blk == 0 and nb <= hd
    t = n_seq * s_len
    grp = 4 if nb % 4 == 0 else 1
    db, nbp, ppb, npg, nstep = _ksum_plan(page_table, cache_k)
    ksteps = db * nstep
    fuse = ksteps <= n_seq * n_heads * nb
    if not fuse:
        npg = 0

    def ks_pos(n, h, i):
        step = jnp.minimum((n * n_heads + h) * nb + i, ksteps - 1)
        return step // nstep, step % nstep

    def page_spec(r):
        def imap(n, h, i, pt):
            b, s = ks_pos(n, h, i)
            return (pt[b, s * npg + r], 0, 0, 0)
        return pl.BlockSpec((1,) + cache_k.shape[1:], imap)

    def ks_map(n, h, i, pt):
        b, s = ks_pos(n, h, i)
        return (b, s, 0, 0)

    out_specs = [pl.BlockSpec((blk, hd), lambda n, h, i, pt: (n * nb + i, h))]
    out_shape = [jax.ShapeDtypeStruct((t, n_heads * hd), BF16)]
    if fuse:
        out_specs.append(pl.BlockSpec((1, npg // ppb) + cache_k.shape[2:], ks_map))
        out_shape.append(jax.ShapeDtypeStruct((db, nbp) + cache_k.shape[2:], F32))
    outs = pl.pallas_call(
        functools.partial(_attn_kernel, blk=blk, nb=nb, grp=grp, scale=1.0 / math.sqrt(hd),
                          npg=npg, ppb=ppb, ksteps=ksteps),
        grid_spec=pltpu.PrefetchScalarGridSpec(
            num_scalar_prefetch=1, grid=(n_seq, n_heads, nb),
            in_specs=[pl.BlockSpec((blk, hd), lambda n, h, i, pt: (n * nb + i, h)),
                      pl.BlockSpec((s_len, hd), lambda n, h, i, pt: (n, h)),
                      pl.BlockSpec((s_len, hd), lambda n, h, i, pt: (n, h))] + [page_spec(r) for r in range(npg)],
            out_specs=out_specs,
            scratch_shapes=[pltpu.VMEM((s_len, 2 * hd), BF16), pltpu.VMEM((s_len, hd), BF16),
                            pltpu.VMEM((hd, hd), F32), pltpu.VMEM((nb // grp, blk, grp * blk), F32)]),
        out_shape=out_shape,
        compiler_params=_cparams(3),
        name="moba_prefill",
    )(page_table, q, k, v, *([cache_k] * npg))
    return (outs[0], outs[1]) if fuse else (outs[0], None)


def _dec_ksum_kernel(pt_ref, *refs, npg, ppb):
    del pt_ref
    _page_sums(refs[:npg], refs[npg], ppb)


def _dec_sel_kernel(ks_ref, q_ref, o_ref, *, blk):
    km = ks_ref[0] * (1.0 / blk)
    nbp, n_heads, _ = km.shape
    g = jnp.sum(km * q_ref[...], axis=2, keepdims=True)
    blk_id = lax.broadcasted_iota(I32, g.shape, 0)
    lane = lax.broadcasted_iota(I32, (n_heads, LANES), 1)
    out = jnp.full((n_heads, LANES), nbp, I32)
    for r in range(MOBA_TOPK):
        m = jnp.max(g, axis=0, keepdims=True)
        idx = jnp.min(jnp.where(g == m, blk_id, nbp), axis=0, keepdims=True)
        pick = jnp.where(m > -jnp.inf, idx, nbp)[0]
        out = jnp.where(lane == r, pick, out)
        g = jnp.where(blk_id == idx, -jnp.inf, g)
    o_ref[0] = out


def _dec_attn_kernel(pt_ref, sel_ref, q_ref, kn_ref, vn_ref, ck_ref, cv_ref, o_ref, kbuf, vbuf, sem,
                     *, nsel, ppb, nbp, scale):
    b = pl.program_id(0)
    nb = pl.num_programs(0)
    n_heads = q_ref.shape[1]
    npg = nsel * ppb

    def copies(bb, slot):
        out = []
        for h in range(n_heads):
            for r in range(nsel):
                blk_id = jnp.minimum(sel_ref[(bb * n_heads + h) * nsel + r], nbp - 1)
                for u in range(ppb):
                    phys = pt_ref[bb, blk_id * ppb + u]
                    j = h * npg + r * ppb + u
                    out.append(pltpu.make_async_copy(ck_ref.at[phys, :, h, :], kbuf.at[slot, j], sem.at[slot]))
                    out.append(pltpu.make_async_copy(cv_ref.at[phys, :, h, :], vbuf.at[slot, j], sem.at[slot]))
        return out

    slot = lax.rem(b, 2)

    @pl.when(b == 0)
    def _():
        for cp in copies(b, 0):
            cp.start()

    @pl.when(b + 1 < nb)
    def _():
        for cp in copies(b + 1, 1 - slot):
            cp.start()

    for cp in copies(b, slot):
        cp.wait()

    q8 = q_ref[0]
    page = kbuf.shape[2]
    for h in range(n_heads):
        ks = jnp.concatenate([kbuf[slot, h * npg + j] for j in range(npg)], axis=0)
        vs = jnp.concatenate([vbuf[slot, h * npg + j] for j in range(npg)], axis=0)
        s_all = lax.dot_general(q8.astype(BF16), ks.astype(BF16), (((1,), (1,)), ((), ())),
                                preferred_element_type=F32)
        s = s_all[h:h + 1] * scale
        col = lax.broadcasted_iota(I32, s.shape, 1)
        ok = jnp.zeros(s.shape, jnp.bool_)
        for r in range(nsel):
            in_seg = (col >= r * ppb * page) & (col < (r + 1) * ppb * page)
            ok = ok | (in_seg & (sel_ref[(b * n_heads + h) * nsel + r] < nbp))
        s = jnp.where(ok, s, -jnp.inf)
        qh, knh, vnh = q8[h:h + 1], kn_ref[0, h:h + 1], vn_ref[0, h:h + 1]
        s_new = jnp.sum(qh * knh, axis=1, keepdims=True) * scale
        m = jnp.maximum(jnp.max(s, axis=1, keepdims=True), s_new)
        p = jnp.exp(s - m)
        p_new = jnp.exp(s_new - m)
        den = jnp.sum(p, axis=1, keepdims=True) + p_new
        out = jnp.dot(p.astype(BF16), vs.astype(BF16), preferred_element_type=F32) + p_new * vnh
        o_ref[0, h:h + 1, :] = out / den


def _moba_decode(q, k_new, v_new, cache_k, cache_v, page_table, ksum):
    _, page, n_heads, hd = cache_k.shape
    blk = MOBA_BLOCK
    db, nbp, ppb, npg, nstep = _ksum_plan(page_table, cache_k)

    if ksum is None:
        def page_spec(r):
            return pl.BlockSpec((1, page, n_heads, hd), lambda b, s, pt: (pt[b, s * npg + r], 0, 0, 0))

        ksum = pl.pallas_call(
            functools.partial(_dec_ksum_kernel, npg=npg, ppb=ppb),
            grid_spec=pltpu.PrefetchScalarGridSpec(
                num_scalar_prefetch=1, grid=(db, nstep),
                in_specs=[page_spec(r) for r in range(npg)],
                out_specs=pl.BlockSpec((1, npg // ppb, n_heads, hd), lambda b, s, pt: (b, s, 0, 0))),
            out_shape=jax.ShapeDtypeStruct((db, nbp, n_heads, hd), F32),
            compiler_params=_cparams(2),
            name="dec_ksum",
        )(page_table, *([cache_k] * npg))

    sel = pl.pallas_call(
        functools.partial(_dec_sel_kernel, blk=blk),
        grid=(db,),
        in_specs=[pl.BlockSpec((1, nbp, n_heads, hd), lambda b: (b, 0, 0, 0)),
                  pl.BlockSpec((1, n_heads, hd), lambda b: (b, 0, 0))],
        out_specs=pl.BlockSpec((1, n_heads, LANES), lambda b: (b, 0, 0)),
        out_shape=jax.ShapeDtypeStruct((db, n_heads, LANES), I32),
        compiler_params=_cparams(1),
        name="dec_select",
    )(ksum, q)
    nsel = MOBA_TOPK
    sel = sel[:, :, :nsel].reshape(-1)

    row_spec = pl.BlockSpec((1, n_heads, hd), lambda b, pt, sl: (b, 0, 0))
    n_buf = n_heads * nsel * ppb
    return pl.pallas_call(
        functools.partial(_dec_attn_kernel, nsel=nsel, ppb=ppb, nbp=nbp, scale=1.0 / math.sqrt(hd)),
        grid_spec=pltpu.PrefetchScalarGridSpec(
            num_scalar_prefetch=2, grid=(db,),
            in_specs=[row_spec, row_spec, row_spec,
                      pl.BlockSpec(memory_space=pl.ANY), pl.BlockSpec(memory_space=pl.ANY)],
            out_specs=row_spec,
            scratch_shapes=[pltpu.VMEM((2, n_buf, page, hd), F32), pltpu.VMEM((2, n_buf, page, hd), F32),
                            pltpu.SemaphoreType.DMA((2,))]),
        out_shape=jax.ShapeDtypeStruct((db, n_heads, hd), F32),
        compiler_params=_cparams(1),
        name="dec_attend",
    )(page_table, sel, q, k_new, v_new, cache_k, cache_v)


def _s5_discretise(lam_re, lam_im, log_dt, b_re, b_im):
    dt = jnp.exp(log_dt.astype(F32))[:, None]
    mag = jnp.exp(lam_re * dt)
    ang = lam_im * dt
    a_re = mag * jnp.cos(ang)
    a_im = mag * jnp.sin(ang)
    den = lam_re * lam_re + lam_im * lam_im
    nr = a_re - 1.0
    f_re = (nr * lam_re + a_im * lam_im) / den
    f_im = (a_im * lam_re - nr * lam_im) / den
    bb_re = f_re[..., None] * b_re - f_im[..., None] * b_im
    bb_im = f_re[..., None] * b_im + f_im[..., None] * b_re
    return a_re, a_im, bb_re, bb_im


def _s5_layout(a_re, a_im, bb_re, bb_im, c_re, c_im, d_skip, n_rows):
    g, p, gc = bb_re.shape
    gb = max(1, min(g // 2, (2 * LANES) // gc, 16))
    nblk = g // gb
    eye = jnp.eye(gb, dtype=F32)

    def in_blocks(bb):
        return jnp.einsum('bgpc,gh->bgchp', bb.reshape(nblk, gb, p, gc), eye).reshape(nblk, gb * gc, gb * p)

    def out_blocks(c):
        return jnp.einsum('bgcp,gh->bgphc', c.reshape(nblk, gb, gc, p), eye).reshape(nblk, gb * p, gb * gc)

    bb = jnp.stack([in_blocks(bb_re), in_blocks(bb_im)]).astype(BF16)
    cc = jnp.concatenate([out_blocks(c_re), -out_blocks(c_im)], axis=1).astype(BF16)
    sh = g * p // 2
    a = jnp.stack([a_re.reshape(2, sh), a_im.reshape(2, sh)])
    a = jnp.repeat(a, n_rows // 2, axis=1)
    a = a.reshape(2, n_rows, sh // LANES, LANES).transpose(0, 2, 1, 3)
    return bb, cc, a, d_skip.reshape(1, g * gc).astype(F32), gb


def _s5_kernel(u_ref, bb_ref, cc_ref, a_ref, d_ref, z_ref, hre_ref, him_ref, xre, xim, hst,
               *, n_seq, length, pitch, nb_half, chains):
    c = pl.program_id(0)
    nrow = 2 * n_seq
    _, nblk, cb, sb = bb_ref.shape
    nsl = sb // LANES
    nslab = nb_half * nsl

    @pl.when(c == 0)
    def _():
        hst[...] = jnp.zeros_like(hst)

    u = u_ref[...].reshape(n_seq * length, u_ref.shape[2])
    ub = u.astype(BF16)
    for half in range(2):
        for b2 in range(nb_half):
            b = half * nb_half + b2
            lhs = ub[:, b * cb:(b + 1) * cb]
            for ri, xs in ((0, xre), (1, xim)):
                res = jnp.dot(lhs, bb_ref[ri, b], preferred_element_type=F32)
                for n in range(n_seq):
                    for s8 in range(nsl):
                        xs[b2 * nsl + s8, pl.ds((half * n_seq + n) * pitch, length), :] = (
                            res[n * length:(n + 1) * length, s8 * LANES:(s8 + 1) * LANES])

    for cg in range(nslab // chains):
        slabs = [cg * chains + kk for kk in range(chains)]
        ar = [a_ref[0, sl] for sl in slabs]
        ai = [a_ref[1, sl] for sl in slabs]
        init = (tuple(hst[0, sl] for sl in slabs), tuple(hst[1, sl] for sl in slabs))

        def step(t, carry, slabs=slabs, ar=ar, ai=ai):
            hr, hi = carry
            nr, ni = [], []
            for kk, sl in enumerate(slabs):
                rows = pl.ds(t, nrow, stride=pitch)
                xr = xre[sl, rows, :]
                xi = xim[sl, rows, :]
                r = ar[kk] * hr[kk] - ai[kk] * hi[kk] + xr
                i = ar[kk] * hi[kk] + ai[kk] * hr[kk] + xi
                xre[sl, rows, :] = r
                xim[sl, rows, :] = i
                nr.append(r)
                ni.append(i)
            return tuple(nr), tuple(ni)

        hr, hi = lax.fori_loop(0, length, step, init)
        for kk, sl in enumerate(slabs):
            hst[0, sl] = hr[kk]
            hst[1, sl] = hi[kk]

    for half in range(2):
        for b2 in range(nb_half):
            b = half * nb_half + b2

            def gather(xs):
                return jnp.concatenate(
                    [jnp.concatenate([xs[b2 * nsl + s8, pl.ds((half * n_seq + n) * pitch, length), :]
                                      for s8 in range(nsl)], axis=1) for n in range(n_seq)], axis=0)

            lhs = jnp.concatenate([gather(xre), gather(xim)], axis=1).astype(BF16)
            y = jnp.dot(lhs, cc_ref[b], preferred_element_type=F32)
            y = y + d_ref[:, b * cb:(b + 1) * cb] * u[:, b * cb:(b + 1) * cb]
            zz = _gelu_tanh(y).astype(z_ref.dtype)
            for n in range(n_seq):
                z_ref[n, :, b * cb:(b + 1) * cb] = zz[n * length:(n + 1) * length]

    @pl.when(c == pl.num_programs(0) - 1)
    def _():
        for sl in range(nslab):
            hre_ref[:, sl * LANES:(sl + 1) * LANES] = hst[0, sl]
            him_ref[:, sl * LANES:(sl + 1) * LANES] = hst[1, sl]


def _s5_prompt(u, s5p, n_seq, s_len):
    a_re, a_im, bb_re, bb_im, c_re, c_im, d_skip = s5p
    g, p, gc = bb_re.shape
    nrow = 2 * n_seq
    assert nrow == SUBLANES, "the scan packs (half, sequence) on the eight sublanes"
    bb, cc, a, d, gb = _s5_layout(a_re, a_im, bb_re, bb_im, c_re, c_im, d_skip, nrow)
    nblk = g // gb
    nb_half = nblk // 2
    sb = gb * p
    nslab = nb_half * sb // LANES
    sh = g * p // 2
    length = min(128, s_len)
    pitch = length + SUBLANES
    chains = 4 if nslab % 4 == 0 else 1
    d_ssm = g * gc
    z, hre, him = pl.pallas_call(
        functools.partial(_s5_kernel, n_seq=n_seq, length=length, pitch=pitch, nb_half=nb_half, chains=chains),
        grid=(s_len // length,),
        in_specs=[pl.BlockSpec((n_seq, length, d_ssm), lambda c: (0, c, 0)),
                  pl.BlockSpec(bb.shape, lambda c: (0, 0, 0, 0)),
                  pl.BlockSpec(cc.shape, lambda c: (0, 0, 0)),
                  pl.BlockSpec(a.shape, lambda c: (0, 0, 0, 0)),
                  pl.BlockSpec(d.shape, lambda c: (0, 0))],
        out_specs=[pl.BlockSpec((n_seq, length, d_ssm), lambda c: (0, c, 0)),
                   pl.BlockSpec((nrow, sh), lambda c: (0, 0)),
                   pl.BlockSpec((nrow, sh), lambda c: (0, 0))],
        out_shape=[jax.ShapeDtypeStruct((n_seq, s_len, d_ssm), BF16),
                   jax.ShapeDtypeStruct((nrow, sh), F32),
                   jax.ShapeDtypeStruct((nrow, sh), F32)],
        scratch_shapes=[pltpu.VMEM((nslab, nrow * pitch, LANES), F32),
                        pltpu.VMEM((nslab, nrow * pitch, LANES), F32),
                        pltpu.VMEM((2, nslab, nrow, LANES), F32)],
        compiler_params=_cparams(1),
        name="s5_prompt",
    )(u, bb, cc, a, d)

    def unpack(h):
        return h.reshape(2, n_seq, g // 2, p).transpose(1, 0, 2, 3).reshape(n_seq, g, p)

    return z, unpack(hre), unpack(him)


def _s5_step_kernel(u_ref, h0r_ref, h0i_ref, bb_ref, cc_ref, ar_ref, ai_ref, d_ref, z_ref, hr_ref, hi_ref):
    _, nblk, cb, sb = bb_ref.shape
    u = u_ref[...]
    ub = u.astype(BF16)
    for b in range(nblk):
        lhs = ub[:, b * cb:(b + 1) * cb]
        st = slice(b * sb, (b + 1) * sb)
        x_re = jnp.dot(lhs, bb_ref[0, b], preferred_element_type=F32)
        x_im = jnp.dot(lhs, bb_ref[1, b], preferred_element_type=F32)
        a_re, a_im = ar_ref[:, st], ai_ref[:, st]
        h0r, h0i = h0r_ref[:, st], h0i_ref[:, st]
        h_re = a_re * h0r - a_im * h0i + x_re
        h_im = a_re * h0i + a_im * h0r + x_im
        hr_ref[:, st] = h_re
        hi_ref[:, st] = h_im
        lhs2 = jnp.concatenate([h_re, h_im], axis=1).astype(BF16)
        y = jnp.dot(lhs2, cc_ref[b], preferred_element_type=F32)
        y = y + d_ref[:, b * cb:(b + 1) * cb] * u[:, b * cb:(b + 1) * cb]
        z_ref[:, b * cb:(b + 1) * cb] = _gelu_tanh(y).astype(z_ref.dtype)


def _s5_sample(u, h0_re, h0_im, s5p):
    a_re, a_im, bb_re, bb_im, c_re, c_im, d_skip = s5p
    g, p, gc = bb_re.shape
    db = u.shape[0]
    bb, cc, _, d, _ = _s5_layout(a_re, a_im, bb_re, bb_im, c_re, c_im, d_skip, SUBLANES)
    ns = g * p
    z, hr, hi = pl.pallas_call(
        _s5_step_kernel,
        out_shape=[jax.ShapeDtypeStruct((db, g * gc), BF16),
                   jax.ShapeDtypeStruct((db, ns), F32),
                   jax.ShapeDtypeStruct((db, ns), F32)],
        compiler_params=pltpu.CompilerParams(vmem_limit_bytes=VMEM_LIMIT_BYTES),
        name="s5_sample",
    )(u, h0_re.reshape(db, ns), h0_im.reshape(db, ns), bb, cc,
      a_re.reshape(1, ns), a_im.reshape(1, ns), d)
    return z, hr.reshape(db, g, p), hi.reshape(db, g, p)


def _mix_kernel(z_ref, at_ref, ga_ref, gb_ref, x_ref, wv_ref, wg_ref, wup_ref, wo_ref, g1_ref, b1_ref,
                wr_ref, br_ref, h_ref, route_ref, *, alpha, n_exp):
    z = z_ref[...]
    br_ssm = (jnp.dot(z, wv_ref[...], preferred_element_type=F32)
              * _sigmoid(jnp.dot(z, wg_ref[...], preferred_element_type=F32)))
    br_attn = jnp.dot(at_ref[...].astype(BF16), wup_ref[...], preferred_element_type=F32)
    mix = _sigmoid(ga_ref[...]) * br_ssm + _sigmoid(gb_ref[...]) * br_attn
    r = alpha * x_ref[...] + jnp.dot(mix.astype(BF16), wo_ref[...], preferred_element_type=F32)
    h = _layernorm(r, g1_ref[...], b1_ref[...])
    h_ref[...] = h
    logits = jnp.dot(h.astype(BF16), wr_ref[...], preferred_element_type=F32) + br_ref[...]
    lane = lax.broadcasted_iota(I32, logits.shape, 1)
    lg = jnp.where(lane < n_exp, logits, -jnp.inf)
    vals, ids = [], []
    for _ in range(TOP_K):
        m = jnp.max(lg, axis=1, keepdims=True)
        idx = jnp.min(jnp.where(lg == m, lane, LANES), axis=1, keepdims=True)
        vals.append(m)
        ids.append(idx)
        lg = jnp.where(lane == idx, -jnp.inf, lg)
    ex = [jnp.exp(v - vals[0]) for v in vals]
    den = ex[0]
    for e in ex[1:]:
        den = den + e
    route = jnp.zeros(logits.shape, F32)
    for kk in range(TOP_K):
        route = jnp.where(lane == kk, ids[kk].astype(F32), route)
        route = jnp.where(lane == TOP_K + kk, ex[kk] / den, route)
    route_ref[...] = route


def _mix(z, attn, ga, gb, x, wts, alpha, n_exp, tm):
    t, d = x.shape
    tm = min(tm, t)
    wv, wg, wup, wo, g1, b1, wr, br = wts

    def rows(w):
        return pl.BlockSpec((tm, w), lambda i: (i, 0))

    def whole(a):
        return pl.BlockSpec(a.shape, lambda i: (0,) * a.ndim, pipeline_mode=pl.Buffered(1))

    return pl.pallas_call(
        functools.partial(_mix_kernel, alpha=alpha, n_exp=n_exp),
        grid=(t // tm,),
        in_specs=[rows(z.shape[1]), rows(attn.shape[1]), rows(d), rows(d), rows(d)] + [whole(w) for w in wts],
        out_specs=[rows(d), rows(LANES)],
        out_shape=[jax.ShapeDtypeStruct((t, d), F32), jax.ShapeDtypeStruct((t, LANES), F32)],
        compiler_params=_cparams(1),
        name="mix_ln1_router",
    )(z, attn, ga, gb, x, *wts)


def _rank_kernel(r_ref, rank_ref, cnt_ref, carry):
    @pl.when(pl.program_id(0) == 0)
    def _():
        carry[...] = jnp.zeros_like(carry)

    r = r_ref[...]
    tm = r.shape[0]
    lane = lax.broadcasted_iota(I32, r.shape, 1)
    ids = [r[:, kk:kk + 1].astype(I32) for kk in range(TOP_K)]
    onehot = jnp.zeros(r.shape, F32)
    for e in ids:
        onehot = onehot + (lane == e).astype(F32)
    tri = (lax.broadcasted_iota(I32, (tm, tm), 0) > lax.broadcasted_iota(I32, (tm, tm), 1)).astype(BF16)
    before = jnp.dot(tri, onehot.astype(BF16), preferred_element_type=F32) + carry[...]
    out = jnp.zeros(r.shape, F32)
    for kk, e in enumerate(ids):
        rk = jnp.sum(jnp.where(lane == e, before, 0.0), axis=1, keepdims=True)
        out = jnp.where(lane == kk, rk, out)
    rank_ref[...] = out.astype(I32)
    carry[...] = carry[...] + jnp.sum(onehot, axis=0, keepdims=True)
    cnt_ref[...] = carry[...]


def _ranks(route_all, tm):
    t = route_all.shape[0]
    return pl.pallas_call(
        _rank_kernel,
        grid=(t // tm,),
        in_specs=[pl.BlockSpec((tm, LANES), lambda i: (i, 0))],
        out_specs=[pl.BlockSpec((tm, LANES), lambda i: (i, 0)), pl.BlockSpec((1, LANES), lambda i: (0, 0))],
        out_shape=[jax.ShapeDtypeStruct((t, LANES), I32), jax.ShapeDtypeStruct((1, LANES), F32)],
        scratch_shapes=[pltpu.VMEM((1, LANES), F32)],
        compiler_params=_cparams(1),
        name="moe_rank",
    )(route_all)


def _dispatch_kernel(dest_ref, h_ref, xs_in, xs_out, sem):
    del xs_in
    tm = h_ref.shape[0]

    def row_copy(r, d):
        return pltpu.make_async_copy(h_ref.at[pl.ds(r, 1), :], xs_out.at[pl.ds(d, 1), :], sem)

    def issue(r, _):
        for kk in range(TOP_K):
            row_copy(r, dest_ref[r * TOP_K + kk]).start()
        return 0

    lax.fori_loop(0, tm, issue, 0)

    def drain(r, _):
        for _kk in range(TOP_K):
            row_copy(0, 0).wait()
        return 0

    lax.fori_loop(0, tm, drain, 0)


def _dispatch(h, dest, xs, tm):
    t, d = h.shape
    tm = min(tm, t)
    return pl.pallas_call(
        _dispatch_kernel,
        grid=(t // tm,),
        in_specs=[pl.BlockSpec((tm * TOP_K,), lambda i: (i,), memory_space=pltpu.SMEM),
                  pl.BlockSpec((tm, d), lambda i: (i, 0)),
                  pl.BlockSpec(memory_space=pl.ANY)],
        out_specs=pl.BlockSpec(memory_space=pl.ANY),
        out_shape=jax.ShapeDtypeStruct(xs.shape, xs.dtype),
        scratch_shapes=[pltpu.SemaphoreType.DMA(())],
        input_output_aliases={2: 0},
        compiler_params=_cparams(1),
        name="moe_dispatch",
    )(dest, h, xs)


def _expert_weights(be_ref, nu, i, f, nf, fetch, land, wb):
    e = be_ref[i]
    first = (i == 0) | (e != be_ref[jnp.maximum(i - 1, 0)])

    @pl.when(first)
    def _():
        @pl.when((i == 0) & (f == 0))
        def _():
            for cp in fetch(e, f):
                cp.start()

        for cp in fetch(e, f):
            cp.wait()
        for m in range(land.shape[0]):
            wb[m] = land[m].astype(BF16)

        last = be_ref.shape[0] - 1
        nxt = lax.while_loop(lambda j: (j < nu) & (be_ref[jnp.minimum(j, last)] == e), lambda j: j + 1, i + 1)

        @pl.when(nxt < nu)
        def _():
            for cp in fetch(be_ref[jnp.minimum(nxt, last)], f):
                cp.start()

        @pl.when((nxt >= nu) & (f + 1 < nf))
        def _():
            for cp in fetch(be_ref[0], f + 1):
                cp.start()


def _g1_kernel(be_ref, nu_ref, xs_ref, bg_ref, bu_ref, wg_hbm, wu_hbm, act_ref, land, wb, sem):
    f = pl.program_id(0)
    i = pl.program_id(1)
    nu = nu_ref[0]
    tf = land.shape[2]

    def fetch(e, ff):
        c0 = pl.multiple_of(ff * tf, tf)
        return (pltpu.make_async_copy(wg_hbm.at[e, :, pl.ds(c0, tf)], land.at[0], sem.at[0]),
                pltpu.make_async_copy(wu_hbm.at[e, :, pl.ds(c0, tf)], land.at[1], sem.at[1]))

    @pl.when(i < nu)
    def _():
        _expert_weights(be_ref, nu, i, f, pl.num_programs(0), fetch, land, wb)
        x = xs_ref[...].astype(BF16)
        g = jnp.dot(x, wb[0], preferred_element_type=F32) + bg_ref[0]
        up = jnp.dot(x, wb[1], preferred_element_type=F32) + bu_ref[0]
        g = jnp.minimum(g, SWIGLU_LIMIT)
        up = jnp.clip(up, -SWIGLU_LIMIT, SWIGLU_LIMIT)
        act_ref[...] = ((up + 1.0) * (g * _sigmoid(SWIGLU_ALPHA * g))).astype(act_ref.dtype)

    @pl.when(i >= nu)
    def _():
        act_ref[...] = jnp.zeros_like(act_ref)


def _g2_kernel(be_ref, nu_ref, act_ref, bd_ref, wd_hbm, y_ref, land, wb, sem):
    i = pl.program_id(0)
    nu = nu_ref[0]

    def fetch(e, ff):
        del ff
        return (pltpu.make_async_copy(wd_hbm.at[e], land.at[0], sem.at[0]),)

    @pl.when(i < nu)
    def _():
        _expert_weights(be_ref, nu, i, 0, 1, fetch, land, wb)
        y_ref[...] = jnp.dot(act_ref[...], wb[0], preferred_element_type=F32) + bd_ref[0]

    @pl.when(i >= nu)
    def _():
        y_ref[...] = jnp.zeros_like(y_ref)


def _experts(xs, blk_e, n_used, w_gate, b_gate, w_up, b_up, w_down, b_down):
    n_rows, d = xs.shape
    n_exp, _, dff = w_gate.shape
    nblk = n_rows // MOE_BLOCK
    tf = min(1024, dff)
    nf = dff // tf

    def blk_row(f, i, be, nu):
        return (jnp.minimum(i, nu[0] - 1), 0)

    def b_spec(f, i, be, nu):
        return (be[jnp.minimum(i, nu[0] - 1)], 0, f)

    hbm = pl.BlockSpec(memory_space=pl.ANY)
    act = pl.pallas_call(
        _g1_kernel,
        grid_spec=pltpu.PrefetchScalarGridSpec(
            num_scalar_prefetch=2, grid=(nf, nblk),
            in_specs=[pl.BlockSpec((MOE_BLOCK, d), blk_row),
                      pl.BlockSpec((1, 1, tf), b_spec), pl.BlockSpec((1, 1, tf), b_spec), hbm, hbm],
            out_specs=pl.BlockSpec((MOE_BLOCK, tf), lambda f, i, be, nu: (i, f)),
            scratch_shapes=[pltpu.VMEM((2, d, tf), F32), pltpu.VMEM((2, d, tf), BF16),
                            pltpu.SemaphoreType.DMA((2,))]),
        out_shape=jax.ShapeDtypeStruct((n_rows, dff), BF16),
        compiler_params=_cparams(2),
        name="moe_gate_up",
    )(blk_e, n_used, xs, b_gate.reshape(n_exp, 1, dff), b_up.reshape(n_exp, 1, dff), w_gate, w_up)

    return pl.pallas_call(
        _g2_kernel,
        grid_spec=pltpu.PrefetchScalarGridSpec(
            num_scalar_prefetch=2, grid=(nblk,),
            in_specs=[pl.BlockSpec((MOE_BLOCK, dff), lambda i, be, nu: (jnp.minimum(i, nu[0] - 1), 0)),
                      pl.BlockSpec((1, 1, d), lambda i, be, nu: (be[jnp.minimum(i, nu[0] - 1)], 0, 0)), hbm],
            out_specs=pl.BlockSpec((MOE_BLOCK, d), lambda i, be, nu: (i, 0)),
            scratch_shapes=[pltpu.VMEM((1, dff, d), F32), pltpu.VMEM((1, dff, d), BF16),
                            pltpu.SemaphoreType.DMA((1,))]),
        out_shape=jax.ShapeDtypeStruct((n_rows, d), F32),
        compiler_params=_cparams(1),
        name="moe_down",
    )(blk_e, n_used, act, b_down.reshape(n_exp, 1, d), w_down)


def _combine_kernel(dest_ref, dest_next_ref, h_ref, route_ref, g_ref, b_ref, yb_ref, o_ref, buf, sem, *, alpha):
    tm = h_ref.shape[0]
    i = pl.program_id(0)
    slot = lax.rem(i, 2)

    def row_copy(r, kk, d, s):
        return pltpu.make_async_copy(yb_ref.at[pl.ds(d, 1), :], buf.at[s, kk, pl.ds(r, 1), :], sem.at[s])

    def issue(idx_ref, s):
        def body(r, _):
            for kk in range(TOP_K):
                row_copy(r, kk, idx_ref[r * TOP_K + kk], s).start()
            return 0
        lax.fori_loop(0, tm, body, 0)

    @pl.when(i == 0)
    def _():
        issue(dest_ref, 0)

    @pl.when(i + 1 < pl.num_programs(0))
    def _():
        issue(dest_next_ref, 1 - slot)

    def drain(r, _):
        for kk in range(TOP_K):
            row_copy(0, kk, 0, slot).wait()
        return 0

    lax.fori_loop(0, tm, drain, 0)
    route = route_ref[...]
    acc = alpha * h_ref[...]
    for kk in range(TOP_K):
        acc = acc + route[:, TOP_K + kk:TOP_K + kk + 1] * buf[slot, kk]
    o_ref[...] = _layernorm(acc, g_ref[...], b_ref[...])


def _combine(h, route, dest, yb, g2, b2, alpha, tm):
    t, d = h.shape
    tm = min(tm, t)
    nt = t // tm
    return pl.pallas_call(
        functools.partial(_combine_kernel, alpha=alpha),
        grid=(nt,),
        in_specs=[pl.BlockSpec((tm * TOP_K,), lambda i: (i,), memory_space=pltpu.SMEM),
                  pl.BlockSpec((tm * TOP_K,), lambda i: (jnp.minimum(i + 1, nt - 1),), memory_space=pltpu.SMEM),
                  pl.BlockSpec((tm, d), lambda i: (i, 0)),
                  pl.BlockSpec((tm, LANES), lambda i: (i, 0)),
                  pl.BlockSpec((1, d), lambda i: (0, 0)),
                  pl.BlockSpec((1, d), lambda i: (0, 0)),
                  pl.BlockSpec(memory_space=pl.ANY)],
        out_specs=pl.BlockSpec((tm, d), lambda i: (i, 0)),
        out_shape=jax.ShapeDtypeStruct((t, d), F32),
        scratch_shapes=[pltpu.VMEM((2, TOP_K, tm, d), F32), pltpu.SemaphoreType.DMA((2,))],
        compiler_params=_cparams(1),
        name="moe_combine_ln2",
    )(dest, dest, h, route, g2, b2, yb)


def _layer(x_prompt, x_sample, cache_k, cache_v, page_table, st_re, st_im, w_in,
           lam_re, lam_im, log_dt, b_re, b_im, c_re, c_im, d_skip,
           w_glu_val, w_glu_gate, w_attn_up, w_o, ln1_g, ln1_b,
           w_router, b_router, w_gate, b_gate, w_up, b_up, w_down, b_down, ln2_g, ln2_b, depth):
    n_seq, s_len, d_model = x_prompt.shape
    db, dec_seq, _ = x_sample.shape
    assert dec_seq == 1, "decode path handles one new token per sequence"
    n_phys, page, n_heads, hd = cache_k.shape
    d_attn = n_heads * hd
    n_groups, n_state = st_re.shape[1:]
    d_ssm = n_groups * b_re.shape[-1]
    n_exp = w_router.shape[1]
    alpha = (2 * depth) ** 0.25
    tp, ts = n_seq * s_len, db
    past = page_table.shape[1] * page
    col_q, col_k, col_v, col_u = 0, d_attn, 2 * d_attn, 3 * d_attn
    col_ga, col_gb = 3 * d_attn + d_ssm, 3 * d_attn + d_ssm + d_model

    wb = w_in.astype(BF16)
    s5p = _s5_discretise(lam_re.astype(F32), lam_im.astype(F32), log_dt, b_re.astype(F32), b_im.astype(F32))
    s5p = s5p + (c_re.astype(F32), c_im.astype(F32), d_skip)
    wr_pad = jnp.zeros((d_model, LANES), BF16).at[:, :n_exp].set(w_router.astype(BF16))
    br_pad = jnp.zeros((1, LANES), F32).at[0, :n_exp].set(b_router.astype(F32))
    mix_w = (w_glu_val.astype(BF16), w_glu_gate.astype(BF16), w_attn_up.astype(BF16), w_o.astype(BF16),
             ln1_g.reshape(1, d_model).astype(F32), ln1_b.reshape(1, d_model).astype(F32), wr_pad, br_pad)

    def project(x2, pos, reps, tm, tag):
        xb = x2.astype(BF16)
        tabs = tuple(jnp.tile(tab, (reps, 1)) for tab in _rope_tables(pos, hd))
        kw = dict(tm=tm, tn=1024)
        q = _proj(xb, wb, col_q, d_attn, BF16 if tag == "p" else F32, rope_tabs=tabs, hd=hd, name=f"proj_q_{tag}", **kw)
        k = _proj(xb, wb, col_k, d_attn, F32, rope_tabs=tabs, hd=hd, name=f"proj_k_{tag}", **kw)
        v = _proj(xb, wb, col_v, d_attn, F32, name=f"proj_v_{tag}", **kw)
        u = _proj(xb, wb, col_u, d_ssm, F32, name=f"proj_u_{tag}", **kw)
        ga = _proj(xb, wb, col_ga, d_model, F32, name=f"proj_ga_{tag}", **kw)
        gb = _proj(xb, wb, col_gb, d_model, F32, name=f"proj_gb_{tag}", **kw)
        return q, k, v, u, ga, gb

    xp2 = x_prompt.reshape(tp, d_model)
    q1, k1, v1, u1, ga1, gb1 = project(xp2, jnp.arange(s_len), n_seq, 512, "p")
    attn1, ksum = _moba_prefill(q1, k1, v1, n_seq, s_len, n_heads, hd, page_table, cache_k)
    z1, hr1, hi1 = _s5_prompt(u1.reshape(n_seq, s_len, d_ssm), s5p, n_seq, s_len)
    h1, route1 = _mix(z1.reshape(tp, d_ssm), attn1, ga1, gb1, xp2, mix_w, alpha, n_exp, 256)

    xs2 = x_sample.reshape(ts, d_model)
    q2, k2, v2, u2, ga2, gb2 = project(xs2, jnp.full((1,), past, I32), ts, ts, "s")
    attn2 = _moba_decode(q2.reshape(ts, n_heads, hd), k2.reshape(ts, n_heads, hd), v2.reshape(ts, n_heads, hd),
                         cache_k, cache_v, page_table, ksum)
    z2, hr2, hi2 = _s5_sample(u2, st_re, st_im, s5p)
    h2, route2 = _mix(z2, attn2.reshape(ts, d_attn), ga2, gb2, xs2, mix_w, alpha, n_exp, ts)

    tr = 256
    t_all = tp + ts
    t_pad = -(-t_all // tr) * tr
    route_all = jnp.concatenate([route1, route2, jnp.full((t_pad - t_all, LANES), -1.0, F32)], axis=0)
    rank_all, cnt = _ranks(route_all, tr)
    counts = cnt[0, :n_exp].astype(I32)
    padded = (counts + MOE_BLOCK - 1) // MOE_BLOCK * MOE_BLOCK
    pad_end = jnp.cumsum(padded)
    pad_start = pad_end - padded
    ids = route_all[:t_all, :TOP_K].astype(I32)
    dest = (pad_start[ids] + rank_all[:t_all, :TOP_K]).astype(I32)
    nblk = -(-(t_all * TOP_K) // MOE_BLOCK) + n_exp
    blk_first_row = jnp.arange(nblk, dtype=I32) * MOE_BLOCK
    blk_e = jnp.minimum(jnp.sum(pad_end[None, :] <= blk_first_row[:, None], axis=1), n_exp - 1).astype(I32)
    n_used = (pad_end[-1:] // MOE_BLOCK).astype(I32)
    dest1 = dest[:tp].reshape(-1)
    dest2 = dest[tp:].reshape(-1)

    xs = jnp.zeros((nblk * MOE_BLOCK, d_model), F32)
    xs = _dispatch(h1, dest1, xs, 256)
    xs = _dispatch(h2, dest2, xs, ts)
    yb = _experts(xs, blk_e, n_used, w_gate, b_gate, w_up, b_up, w_down, b_down)
    g2 = ln2_g.reshape(1, d_model).astype(F32)
    b2 = ln2_b.reshape(1, d_model).astype(F32)
    y1 = _combine(h1, route1, dest1, yb, g2, b2, alpha, 256)
    y2 = _combine(h2, route2, dest2, yb, g2, b2, alpha, ts)

    return (y1.reshape(n_seq, s_len, d_model), y2.reshape(db, 1, d_model),
            k1.reshape(n_seq, s_len, n_heads, hd), v1.reshape(n_seq, s_len, n_heads, hd), hr1, hi1,
            k2.reshape(db, 1, n_heads, hd), v2.reshape(db, 1, n_heads, hd), hr2, hi2)


def kernel(x_prompt, x_sample, cache_k, cache_v, page_table, state_ssm_re, state_ssm_im, w_in, ssm_lambda_re, ssm_lambda_im, ssm_log_dt, ssm_b_re, ssm_b_im, ssm_c_re, ssm_c_im, ssm_d, w_glu_val, w_glu_gate, w_attn_up, w_o, ln1_g, ln1_b, w_router, b_router, w_gate, b_gate, w_up, b_up, w_down, b_down, ln2_g, ln2_b):
    depth = w_in.shape[0]
    assert depth == 1, "single-layer step"
    outs = _layer(x_prompt, x_sample, cache_k[0], cache_v[0], page_table, state_ssm_re[0], state_ssm_im[0],
                  w_in[0], ssm_lambda_re[0], ssm_lambda_im[0], ssm_log_dt[0], ssm_b_re[0], ssm_b_im[0],
                  ssm_c_re[0], ssm_c_im[0], ssm_d[0], w_glu_val[0], w_glu_gate[0], w_attn_up[0], w_o[0],
                  ln1_g[0], ln1_b[0], w_router[0], b_router[0], w_gate[0], b_gate[0], w_up[0], b_up[0],
                  w_down[0], b_down[0], ln2_g[0], ln2_b[0], depth)
    y1, y2, k1, v1, hr1, hi1, k2, v2, hr2, hi2 = outs
    return (y1, y2, k1[None], v1[None], hr1[None], hi1[None], k2[None], v2[None], hr2[None], hi2[None])
```

```python
import functools
import math

import jax
import jax.numpy as jnp
from jax import lax
from jax.experimental import pallas as pl
from jax.experimental.pallas import tpu as pltpu

F32, BF16, I32 = jnp.float32, jnp.bfloat16, jnp.int32

ROPE_THETA = 500000.0
ROT_FRACTION = 4
MOBA_BLOCK = 256
MOBA_TOPK = 3
TOP_K = 4
SWIGLU_LIMIT = 7.0
SWIGLU_ALPHA = 1.702
LN_EPS = 1e-5
MOE_BLOCK = 256

LANES = 128
SUBLANES = 8
VMEM_LIMIT_BYTES = 56 * 1024 * 1024
DMA_PRIORITIES = 2

NEG_BIG = -1e30


def _cparams(n_axes):
    return pltpu.CompilerParams(dimension_semantics=("arbitrary",) * n_axes,
                                vmem_limit_bytes=VMEM_LIMIT_BYTES)


def _sigmoid(x):
    return 1.0 / (1.0 + jnp.exp(-x))


def _gelu_tanh(x):
    c = math.sqrt(2.0 / math.pi)
    return x * (0.5 * (1.0 + jnp.tanh(c * (x + 0.044715 * (x * x * x)))))


def _layernorm(r, g, b):
    mu = jnp.mean(r, axis=-1, keepdims=True)
    xc = r - mu
    var = jnp.mean(xc * xc, axis=-1, keepdims=True)
    return xc * lax.rsqrt(var + LN_EPS) * g + b


def _proj_kernel(x_ref, w_ref, *rest, rope, rot_half, hd):
    acc = jnp.dot(x_ref[...], w_ref[...], preferred_element_type=F32)
    if not rope:
        (o_ref,) = rest
        o_ref[...] = acc.astype(o_ref.dtype)
        return
    cos_ref, sin_ref, o_ref = rest
    cos = cos_ref[...]
    sin = sin_ref[...]
    first = lax.broadcasted_iota(I32, cos.shape, 1) < rot_half
    for hh in range(acc.shape[1] // hd):
        xh = acc[:, hh * hd:(hh + 1) * hd]
        rot = jnp.where(first, pltpu.roll(xh, hd - rot_half, 1), pltpu.roll(xh, rot_half, 1))
        o_ref[:, hh * hd:(hh + 1) * hd] = (xh * cos + rot * sin).astype(o_ref.dtype)


def _proj(xb, wb, col0, ncols, out_dtype, *, tm, tn, rope_tabs=None, hd=LANES, name):
    t, k = xb.shape
    tm = min(tm, t)
    tn = min(tn, ncols)
    off = col0 // tn
    assert col0 % tn == 0 and ncols % tn == 0 and t % tm == 0
    in_specs = [pl.BlockSpec((tm, k), lambda i, j: (i, 0)),
                pl.BlockSpec((k, tn), lambda i, j: (0, j + off))]
    args = [xb, wb]
    if rope_tabs is not None:
        in_specs += [pl.BlockSpec((tm, hd), lambda i, j: (i, 0))] * 2
        args += list(rope_tabs)
    return pl.pallas_call(
        functools.partial(_proj_kernel, rope=rope_tabs is not None, rot_half=hd // ROT_FRACTION // 2, hd=hd),
        grid=(t // tm, ncols // tn),
        in_specs=in_specs,
        out_specs=pl.BlockSpec((tm, tn), lambda i, j: (i, j)),
        out_shape=jax.ShapeDtypeStruct((t, ncols), out_dtype),
        compiler_params=_cparams(2),
        name=name,
    )(*args)


def _rope_tables(pos, hd):
    rot = hd // ROT_FRACTION
    half = rot // 2
    inv = ROPE_THETA ** (-jnp.arange(half, dtype=F32) * 2.0 / rot)
    ang = pos.astype(F32)[:, None] * inv[None, :]
    cos, sin = jnp.cos(ang), jnp.sin(ang)
    t = pos.shape[0]
    cosf = jnp.concatenate([cos, cos, jnp.ones((t, hd - rot), F32)], axis=1)
    sinf = jnp.concatenate([-sin, sin, jnp.zeros((t, hd - rot), F32)], axis=1)
    return cosf, sinf


def _page_sums(page_refs, o_ref, ppb):
    page = page_refs[0].shape[1]
    chains = 16 if page % 16 == 0 else 1
    for r in range(len(page_refs) // ppb):
        x = None
        for u in range(ppb):
            ref = page_refs[r * ppb + u]
            for c in range(page // chains):
                piece = ref[0, c * chains:(c + 1) * chains]
                x = piece if x is None else x + piece
        o_ref[0, r] = jnp.sum(x, axis=0)


def _attn_kernel(pt_ref, q_ref, k_ref, v_ref, *rest, blk, nb, grp, scale, npg, ppb, ksteps):
    del pt_ref
    page_refs = rest[:npg]
    if npg:
        o_ref, ks_ref, ka_ref, vb_ref, km_ref, s_ref = rest[npg:]
        step = (pl.program_id(0) * pl.num_programs(1) + pl.program_id(1)) * pl.num_programs(2) + pl.program_id(2)

        @pl.when(step < ksteps)
        def _():
            _page_sums(page_refs, ks_ref, ppb)
    else:
        o_ref, ka_ref, vb_ref, km_ref, s_ref = rest

    qi = pl.program_id(2)
    s_len, hd = k_ref.shape

    @pl.when(qi == 0)
    def _():
        vb_ref[...] = v_ref[...].astype(BF16)
        ka_ref[:, :hd] = k_ref[...].astype(BF16)
        row = lax.broadcasted_iota(I32, (s_len, hd), 0)
        lane = lax.broadcasted_iota(I32, (s_len, hd), 1)
        member = (row >= lane * blk) & (row < (lane + 1) * blk)
        ka_ref[:, hd:] = jnp.where(member, 1.0, 0.0).astype(BF16)
        km_ref[...] = jnp.zeros_like(km_ref)
        for j in range(nb):
            km_ref[j:j + 1, :] = jnp.mean(k_ref[j * blk:(j + 1) * blk, :], axis=0, keepdims=True)

    q = q_ref[...]
    tq = q.shape[0]
    nt = (((1,), (1,)), ((), ()))
    nbp = -(-nb // SUBLANES) * SUBLANES
    gate_t = lax.dot_general(km_ref[:nbp, :], q.astype(F32), nt,
                             precision=lax.Precision.HIGHEST, preferred_element_type=F32)
    blk_id = lax.broadcasted_iota(I32, (nbp, tq), 0)
    g = jnp.where(blk_id < qi, gate_t, -jnp.inf)
    sel = jnp.zeros(g.shape, jnp.bool_)
    for _ in range(MOBA_TOPK):
        m = jnp.max(g, axis=0, keepdims=True)
        idx = jnp.min(jnp.where(g == m, blk_id, nbp), axis=0, keepdims=True)
        hit = blk_id == idx
        sel = sel | (hit & (m > -jnp.inf))
        g = jnp.where(hit, -jnp.inf, g)
    bias_t = jnp.concatenate([jnp.where(sel, 0.0, NEG_BIG), jnp.full((hd - nbp, tq), NEG_BIG, F32)], axis=0)
    qa = jnp.concatenate([q, bias_t.T.astype(BF16)], axis=1)
    gw = grp * blk
    ntile = blk // LANES

    def lane_fold(x, acc, op):
        for c in range(x.shape[1] // LANES):
            acc = op(acc, x[:, c * LANES:(c + 1) * LANES])
        return acc

    d0 = pl.multiple_of(qi * blk, blk)
    s_own = lax.dot_general(q, ka_ref[pl.ds(d0, blk), :hd], nt, preferred_element_type=F32) * scale
    row = lax.broadcasted_iota(I32, (tq, blk), 0)
    col = lax.broadcasted_iota(I32, (tq, blk), 1)
    s_own = jnp.where(col <= row, s_own, -jnp.inf)
    m_own = lane_fold(s_own[:, LANES:], s_own[:, :LANES], jnp.maximum) if ntile > 1 else s_own

    def attend(ng):
        m_part = m_own
        for gi in range(ng):
            s = lax.dot_general(qa, ka_ref[gi * gw:(gi + 1) * gw, :], nt, preferred_element_type=F32) * scale
            s_ref[gi] = s
            m_part = lane_fold(s, m_part, jnp.maximum)
        m = jnp.max(m_part, axis=1, keepdims=True)
        p_own = jnp.exp(s_own - m)
        l_part = lane_fold(p_own[:, LANES:], p_own[:, :LANES], jnp.add) if ntile > 1 else p_own
        acc = jnp.dot(p_own.astype(BF16), vb_ref[pl.ds(d0, blk), :], preferred_element_type=F32)
        for gi in range(ng):
            p = jnp.exp(s_ref[gi] - m)
            acc = acc + jnp.dot(p.astype(BF16), vb_ref[gi * gw:(gi + 1) * gw, :], preferred_element_type=F32)
            l_part = lane_fold(p, l_part, jnp.add)
        o_ref[...] = (acc / jnp.sum(l_part, axis=1, keepdims=True)).astype(o_ref.dtype)

    ng = lax.div(qi + (grp - 1), grp)
    for n_groups in range((nb - 1 + grp - 1) // grp + 1):
        pl.when(ng == n_groups)(functools.partial(attend, n_groups))


def _ksum_plan(page_table, cache_k):
    db, n_pages = page_table.shape
    page = cache_k.shape[1]
    ppb = MOBA_BLOCK // page
    assert MOBA_BLOCK % page == 0 and n_pages % ppb == 0 and n_pages >= ppb
    npg = 8 if n_pages % 8 == 0 else ppb
    return db, n_pages // ppb, ppb, npg, n_pages // npg


def _moba_prefill(q, k, v, n_seq, s_len, n_heads, hd, page_table, cache_k):
    blk = MOBA_BLOCK
    nb = s_len // blk
    assert s_len % blk == 0 and nb <= hd
    t = n_seq * s_len
    grp = 4 if nb % 4 == 0 else 1
    db, nbp, ppb, npg, nstep = _ksum_plan(page_table, cache_k)
    ksteps = db * nstep
    fuse = ksteps <= n_seq * n_heads * nb
    if not fuse:
        npg = 0

    def ks_step(n, h, i):
        return jnp.minimum((n * n_heads + h) * nb + i, ksteps - 1)

    def page_spec(r):
        return pl.BlockSpec((1,) + cache_k.shape[1:], lambda n, h, i, pg: (pg[ks_step(n, h, i), r], 0, 0, 0))

    out_specs = [pl.BlockSpec((blk, hd), lambda n, h, i, pg: (n * nb + i, h))]
    out_shape = [jax.ShapeDtypeStruct((t, n_heads * hd), BF16)]
    if fuse:
        out_specs.append(pl.BlockSpec((1, npg // ppb) + cache_k.shape[2:],
                                      lambda n, h, i, pg: (ks_step(n, h, i), 0, 0, 0)))
        out_shape.append(jax.ShapeDtypeStruct((ksteps, npg // ppb) + cache_k.shape[2:], F32))
    outs = pl.pallas_call(
        functools.partial(_attn_kernel, blk=blk, nb=nb, grp=grp, scale=1.0 / math.sqrt(hd),
                          npg=npg, ppb=ppb, ksteps=ksteps),
        grid_spec=pltpu.PrefetchScalarGridSpec(
            num_scalar_prefetch=1, grid=(n_seq, n_heads, nb),
            in_specs=[pl.BlockSpec((blk, hd), lambda n, h, i, pt: (n * nb + i, h)),
                      pl.BlockSpec((s_len, hd), lambda n, h, i, pt: (n, h)),
                      pl.BlockSpec((s_len, hd), lambda n, h, i, pt: (n, h))] + [page_spec(r) for r in range(npg)],
            out_specs=out_specs,
            scratch_shapes=[pltpu.VMEM((s_len, 2 * hd), BF16), pltpu.VMEM((s_len, hd), BF16),
                            pltpu.VMEM((hd, hd), F32), pltpu.VMEM((nb // grp, blk, grp * blk), F32)]),
        out_shape=out_shape,
        compiler_params=_cparams(3),
        name="moba_prefill",
    )(page_table.reshape(ksteps, -1), q, k, v, *([cache_k] * npg))
    return (outs[0], outs[1].reshape((db, nbp) + cache_k.shape[2:])) if fuse else (outs[0], None)


def _dec_ksum_kernel(pt_ref, *refs, npg, ppb):
    del pt_ref
    _page_sums(refs[:npg], refs[npg], ppb)


def _dec_sel_kernel(ks_ref, q_ref, o_ref, *, blk):
    km = ks_ref[0] * (1.0 / blk)
    nbp, n_heads, _ = km.shape
    g = jnp.sum(km * q_ref[...], axis=2, keepdims=True)
    blk_id = lax.broadcasted_iota(I32, g.shape, 0)
    lane = lax.broadcasted_iota(I32, (n_heads, LANES), 1)
    out = jnp.full((n_heads, LANES), nbp, I32)
    for r in range(MOBA_TOPK):
        m = jnp.max(g, axis=0, keepdims=True)
        idx = jnp.min(jnp.where(g == m, blk_id, nbp), axis=0, keepdims=True)
        pick = jnp.where(m > -jnp.inf, idx, nbp)[0]
        out = jnp.where(lane == r, pick, out)
        g = jnp.where(blk_id == idx, -jnp.inf, g)
    o_ref[0] = out


def _dec_attn_kernel(pt_ref, sel_ref, q_ref, kn_ref, vn_ref, ck_ref, cv_ref, o_ref, kbuf, vbuf, sem,
                     *, nsel, ppb, nbp, scale):
    b = pl.program_id(0)
    nb = pl.num_programs(0)
    n_heads = q_ref.shape[1]
    npg = nsel * ppb

    def copies(bb, slot):
        out = []
        for h in range(n_heads):
            for r in range(nsel):
                blk_id = jnp.minimum(sel_ref[(bb * n_heads + h) * nsel + r], nbp - 1)
                for u in range(ppb):
                    phys = pt_ref[bb, blk_id * ppb + u]
                    j = h * npg + r * ppb + u
                    out.append(pltpu.make_async_copy(ck_ref.at[phys, :, h, :], kbuf.at[slot, j], sem.at[slot]))
                    out.append(pltpu.make_async_copy(cv_ref.at[phys, :, h, :], vbuf.at[slot, j], sem.at[slot]))
        return out

    slot = lax.rem(b, 2)

    @pl.when(b == 0)
    def _():
        for cp in copies(b, 0):
            cp.start()

    @pl.when(b + 1 < nb)
    def _():
        for cp in copies(b + 1, 1 - slot):
            cp.start()

    for cp in copies(b, slot):
        cp.wait()

    q8 = q_ref[0]
    page = kbuf.shape[2]
    for h in range(n_heads):
        ks = jnp.concatenate([kbuf[slot, h * npg + j] for j in range(npg)], axis=0)
        vs = jnp.concatenate([vbuf[slot, h * npg + j] for j in range(npg)], axis=0)
        s_all = lax.dot_general(q8.astype(BF16), ks.astype(BF16), (((1,), (1,)), ((), ())),
                                preferred_element_type=F32)
        s = s_all[h:h + 1] * scale
        col = lax.broadcasted_iota(I32, s.shape, 1)
        ok = jnp.zeros(s.shape, jnp.bool_)
        for r in range(nsel):
            in_seg = (col >= r * ppb * page) & (col < (r + 1) * ppb * page)
            ok = ok | (in_seg & (sel_ref[(b * n_heads + h) * nsel + r] < nbp))
        s = jnp.where(ok, s, -jnp.inf)
        qh, knh, vnh = q8[h:h + 1], kn_ref[0, h:h + 1], vn_ref[0, h:h + 1]
        s_new = jnp.sum(qh * knh, axis=1, keepdims=True) * scale
        m = jnp.maximum(jnp.max(s, axis=1, keepdims=True), s_new)
        p = jnp.exp(s - m)
        p_new = jnp.exp(s_new - m)
        den = jnp.sum(p, axis=1, keepdims=True) + p_new
        out = jnp.dot(p.astype(BF16), vs.astype(BF16), preferred_element_type=F32) + p_new * vnh
        o_ref[0, h:h + 1, :] = out / den


def _moba_decode(q, k_new, v_new, cache_k, cache_v, page_table, ksum):
    _, page, n_heads, hd = cache_k.shape
    blk = MOBA_BLOCK
    db, nbp, ppb, npg, nstep = _ksum_plan(page_table, cache_k)

    if ksum is None:
        def page_spec(r):
            return pl.BlockSpec((1, page, n_heads, hd), lambda b, s, pt: (pt[b, s * npg + r], 0, 0, 0))

        ksum = pl.pallas_call(
            functools.partial(_dec_ksum_kernel, npg=npg, ppb=ppb),
            grid_spec=pltpu.PrefetchScalarGridSpec(
                num_scalar_prefetch=1, grid=(db, nstep),
                in_specs=[page_spec(r) for r in range(npg)],
                out_specs=pl.BlockSpec((1, npg // ppb, n_heads, hd), lambda b, s, pt: (b, s, 0, 0))),
            out_shape=jax.ShapeDtypeStruct((db, nbp, n_heads, hd), F32),
            compiler_params=_cparams(2),
            name="dec_ksum",
        )(page_table, *([cache_k] * npg))

    sel = pl.pallas_call(
        functools.partial(_dec_sel_kernel, blk=blk),
        grid=(db,),
        in_specs=[pl.BlockSpec((1, nbp, n_heads, hd), lambda b: (b, 0, 0, 0)),
                  pl.BlockSpec((1, n_heads, hd), lambda b: (b, 0, 0))],
        out_specs=pl.BlockSpec((1, n_heads, LANES), lambda b: (b, 0, 0)),
        out_shape=jax.ShapeDtypeStruct((db, n_heads, LANES), I32),
        compiler_params=_cparams(1),
        name="dec_select",
    )(ksum, q)
    nsel = MOBA_TOPK
    sel = sel[:, :, :nsel].reshape(-1)

    row_spec = pl.BlockSpec((1, n_heads, hd), lambda b, pt, sl: (b, 0, 0))
    n_buf = n_heads * nsel * ppb
    return pl.pallas_call(
        functools.partial(_dec_attn_kernel, nsel=nsel, ppb=ppb, nbp=nbp, scale=1.0 / math.sqrt(hd)),
        grid_spec=pltpu.PrefetchScalarGridSpec(
            num_scalar_prefetch=2, grid=(db,),
            in_specs=[row_spec, row_spec, row_spec,
                      pl.BlockSpec(memory_space=pl.ANY), pl.BlockSpec(memory_space=pl.ANY)],
            out_specs=row_spec,
            scratch_shapes=[pltpu.VMEM((2, n_buf, page, hd), F32), pltpu.VMEM((2, n_buf, page, hd), F32),
                            pltpu.SemaphoreType.DMA((2,))]),
        out_shape=jax.ShapeDtypeStruct((db, n_heads, hd), F32),
        compiler_params=_cparams(1),
        name="dec_attend",
    )(page_table, sel, q, k_new, v_new, cache_k, cache_v)


def _s5_discretise(lam_re, lam_im, log_dt, b_re, b_im):
    dt = jnp.exp(log_dt.astype(F32))[:, None]
    mag = jnp.exp(lam_re * dt)
    ang = lam_im * dt
    a_re = mag * jnp.cos(ang)
    a_im = mag * jnp.sin(ang)
    den = lam_re * lam_re + lam_im * lam_im
    nr = a_re - 1.0
    f_re = (nr * lam_re + a_im * lam_im) / den
    f_im = (a_im * lam_re - nr * lam_im) / den
    bb_re = f_re[..., None] * b_re - f_im[..., None] * b_im
    bb_im = f_re[..., None] * b_im + f_im[..., None] * b_re
    return a_re, a_im, bb_re, bb_im


def _s5_layout(a_re, a_im, bb_re, bb_im, c_re, c_im, d_skip, n_rows):
    g, p, gc = bb_re.shape
    gb = max(1, min(g // 2, (2 * LANES) // gc, 16))
    nblk = g // gb
    eye = jnp.eye(gb, dtype=F32)

    def in_blocks(bb):
        return jnp.einsum('bgpc,gh->bgchp', bb.reshape(nblk, gb, p, gc), eye).reshape(nblk, gb * gc, gb * p)

    def out_blocks(c):
        return jnp.einsum('bgcp,gh->bgphc', c.reshape(nblk, gb, gc, p), eye).reshape(nblk, gb * p, gb * gc)

    bb = jnp.stack([in_blocks(bb_re), in_blocks(bb_im)]).astype(BF16)
    cc = jnp.concatenate([out_blocks(c_re), -out_blocks(c_im)], axis=1).astype(BF16)
    sh = g * p // 2
    a = jnp.stack([a_re.reshape(2, sh), a_im.reshape(2, sh)])
    a = jnp.repeat(a, n_rows // 2, axis=1)
    a = a.reshape(2, n_rows, sh // LANES, LANES).transpose(0, 2, 1, 3)
    return bb, cc, a, d_skip.reshape(1, g * gc).astype(F32), gb


def _s5_kernel(u_ref, bb_ref, cc_ref, a_ref, d_ref, z_ref, hre_ref, him_ref, xre, xim, hst,
               *, n_seq, length, pitch, nb_half, chains):
    c = pl.program_id(0)
    nrow = 2 * n_seq
    _, nblk, cb, sb = bb_ref.shape
    nsl = sb // LANES
    nslab = nb_half * nsl

    @pl.when(c == 0)
    def _():
        hst[...] = jnp.zeros_like(hst)

    u = u_ref[...].reshape(n_seq * length, u_ref.shape[2])
    ub = u.astype(BF16)
    for half in range(2):
        for b2 in range(nb_half):
            b = half * nb_half + b2
            lhs = ub[:, b * cb:(b + 1) * cb]
            for ri, xs in ((0, xre), (1, xim)):
                res = jnp.dot(lhs, bb_ref[ri, b], preferred_element_type=F32)
                for n in range(n_seq):
                    for s8 in range(nsl):
                        xs[b2 * nsl + s8, pl.ds((half * n_seq + n) * pitch, length), :] = (
                            res[n * length:(n + 1) * length, s8 * LANES:(s8 + 1) * LANES])

    for cg in range(nslab // chains):
        slabs = [cg * chains + kk for kk in range(chains)]
        ar = [a_ref[0, sl] for sl in slabs]
        ai = [a_ref[1, sl] for sl in slabs]
        init = (tuple(hst[0, sl] for sl in slabs), tuple(hst[1, sl] for sl in slabs))

        def step(t, carry, slabs=slabs, ar=ar, ai=ai):
            hr, hi = carry
            nr, ni = [], []
            for kk, sl in enumerate(slabs):
                rows = pl.ds(t, nrow, stride=pitch)
                xr = xre[sl, rows, :]
                xi = xim[sl, rows, :]
                r = ar[kk] * hr[kk] - ai[kk] * hi[kk] + xr
                i = ar[kk] * hi[kk] + ai[kk] * hr[kk] + xi
                xre[sl, rows, :] = r
                xim[sl, rows, :] = i
                nr.append(r)
                ni.append(i)
            return tuple(nr), tuple(ni)

        hr, hi = lax.fori_loop(0, length, step, init)
        for kk, sl in enumerate(slabs):
            hst[0, sl] = hr[kk]
            hst[1, sl] = hi[kk]

    for half in range(2):
        for b2 in range(nb_half):
            b = half * nb_half + b2

            def gather(xs):
                return jnp.concatenate(
                    [jnp.concatenate([xs[b2 * nsl + s8, pl.ds((half * n_seq + n) * pitch, length), :]
                                      for s8 in range(nsl)], axis=1) for n in range(n_seq)], axis=0)

            lhs = jnp.concatenate([gather(xre), gather(xim)], axis=1).astype(BF16)
            y = jnp.dot(lhs, cc_ref[b], preferred_element_type=F32)
            y = y + d_ref[:, b * cb:(b + 1) * cb] * u[:, b * cb:(b + 1) * cb]
            zz = _gelu_tanh(y).astype(z_ref.dtype)
            for n in range(n_seq):
                z_ref[n, :, b * cb:(b + 1) * cb] = zz[n * length:(n + 1) * length]

    @pl.when(c == pl.num_programs(0) - 1)
    def _():
        for sl in range(nslab):
            hre_ref[:, sl * LANES:(sl + 1) * LANES] = hst[0, sl]
            him_ref[:, sl * LANES:(sl + 1) * LANES] = hst[1, sl]


def _s5_prompt(u, s5p, n_seq, s_len):
    a_re, a_im, bb_re, bb_im, c_re, c_im, d_skip = s5p
    g, p, gc = bb_re.shape
    nrow = 2 * n_seq
    assert nrow == SUBLANES, "the scan packs (half, sequence) on the eight sublanes"
    bb, cc, a, d, gb = _s5_layout(a_re, a_im, bb_re, bb_im, c_re, c_im, d_skip, nrow)
    nblk = g // gb
    nb_half = nblk // 2
    sb = gb * p
    nslab = nb_half * sb // LANES
    sh = g * p // 2
    length = min(128, s_len)
    pitch = length + SUBLANES
    chains = 4 if nslab % 4 == 0 else 1
    d_ssm = g * gc
    z, hre, him = pl.pallas_call(
        functools.partial(_s5_kernel, n_seq=n_seq, length=length, pitch=pitch, nb_half=nb_half, chains=chains),
        grid=(s_len // length,),
        in_specs=[pl.BlockSpec((n_seq, length, d_ssm), lambda c: (0, c, 0)),
                  pl.BlockSpec(bb.shape, lambda c: (0, 0, 0, 0)),
                  pl.BlockSpec(cc.shape, lambda c: (0, 0, 0)),
                  pl.BlockSpec(a.shape, lambda c: (0, 0, 0, 0)),
                  pl.BlockSpec(d.shape, lambda c: (0, 0))],
        out_specs=[pl.BlockSpec((n_seq, length, d_ssm), lambda c: (0, c, 0)),
                   pl.BlockSpec((nrow, sh), lambda c: (0, 0)),
                   pl.BlockSpec((nrow, sh), lambda c: (0, 0))],
        out_shape=[jax.ShapeDtypeStruct((n_seq, s_len, d_ssm), BF16),
                   jax.ShapeDtypeStruct((nrow, sh), F32),
                   jax.ShapeDtypeStruct((nrow, sh), F32)],
        scratch_shapes=[pltpu.VMEM((nslab, nrow * pitch, LANES), F32),
                        pltpu.VMEM((nslab, nrow * pitch, LANES), F32),
                        pltpu.VMEM((2, nslab, nrow, LANES), F32)],
        compiler_params=_cparams(1),
        name="s5_prompt",
    )(u, bb, cc, a, d)

    def unpack(h):
        return h.reshape(2, n_seq, g // 2, p).transpose(1, 0, 2, 3).reshape(n_seq, g, p)

    return z, unpack(hre), unpack(him)


def _s5_step_kernel(u_ref, h0r_ref, h0i_ref, bb_ref, cc_ref, ar_ref, ai_ref, d_ref, z_ref, hr_ref, hi_ref):
    _, nblk, cb, sb = bb_ref.shape
    u = u_ref[...]
    ub = u.astype(BF16)
    for b in range(nblk):
        lhs = ub[:, b * cb:(b + 1) * cb]
        st = slice(b * sb, (b + 1) * sb)
        x_re = jnp.dot(lhs, bb_ref[0, b], preferred_element_type=F32)
        x_im = jnp.dot(lhs, bb_ref[1, b], preferred_element_type=F32)
        a_re, a_im = ar_ref[:, st], ai_ref[:, st]
        h0r, h0i = h0r_ref[:, st], h0i_ref[:, st]
        h_re = a_re * h0r - a_im * h0i + x_re
        h_im = a_re * h0i + a_im * h0r + x_im
        hr_ref[:, st] = h_re
        hi_ref[:, st] = h_im
        lhs2 = jnp.concatenate([h_re, h_im], axis=1).astype(BF16)
        y = jnp.dot(lhs2, cc_ref[b], preferred_element_type=F32)
        y = y + d_ref[:, b * cb:(b + 1) * cb] * u[:, b * cb:(b + 1) * cb]
        z_ref[:, b * cb:(b + 1) * cb] = _gelu_tanh(y).astype(z_ref.dtype)


def _s5_sample(u, h0_re, h0_im, s5p):
    a_re, a_im, bb_re, bb_im, c_re, c_im, d_skip = s5p
    g, p, gc = bb_re.shape
    db = u.shape[0]
    bb, cc, _, d, _ = _s5_layout(a_re, a_im, bb_re, bb_im, c_re, c_im, d_skip, SUBLANES)
    ns = g * p
    z, hr, hi = pl.pallas_call(
        _s5_step_kernel,
        out_shape=[jax.ShapeDtypeStruct((db, g * gc), BF16),
                   jax.ShapeDtypeStruct((db, ns), F32),
                   jax.ShapeDtypeStruct((db, ns), F32)],
        compiler_params=pltpu.CompilerParams(vmem_limit_bytes=VMEM_LIMIT_BYTES),
        name="s5_sample",
    )(u, h0_re.reshape(db, ns), h0_im.reshape(db, ns), bb, cc,
      a_re.reshape(1, ns), a_im.reshape(1, ns), d)
    return z, hr.reshape(db, g, p), hi.reshape(db, g, p)


def _mix_kernel(z_ref, at_ref, ga_ref, gb_ref, x_ref, wv_ref, wg_ref, wup_ref, wo_ref, g1_ref, b1_ref,
                wr_ref, br_ref, h_ref, route_ref, *, alpha, n_exp):
    z = z_ref[...]
    br_ssm = (jnp.dot(z, wv_ref[...], preferred_element_type=F32)
              * _sigmoid(jnp.dot(z, wg_ref[...], preferred_element_type=F32)))
    br_attn = jnp.dot(at_ref[...].astype(BF16), wup_ref[...], preferred_element_type=F32)
    mix = _sigmoid(ga_ref[...]) * br_ssm + _sigmoid(gb_ref[...]) * br_attn
    r = alpha * x_ref[...] + jnp.dot(mix.astype(BF16), wo_ref[...], preferred_element_type=F32)
    h = _layernorm(r, g1_ref[...], b1_ref[...])
    h_ref[...] = h
    logits = jnp.dot(h.astype(BF16), wr_ref[...], preferred_element_type=F32) + br_ref[...]
    lane = lax.broadcasted_iota(I32, logits.shape, 1)
    lg = jnp.where(lane < n_exp, logits, -jnp.inf)
    vals, ids = [], []
    for _ in range(TOP_K):
        m = jnp.max(lg, axis=1, keepdims=True)
        idx = jnp.min(jnp.where(lg == m, lane, LANES), axis=1, keepdims=True)
        vals.append(m)
        ids.append(idx)
        lg = jnp.where(lane == idx, -jnp.inf, lg)
    ex = [jnp.exp(v - vals[0]) for v in vals]
    den = ex[0]
    for e in ex[1:]:
        den = den + e
    route = jnp.zeros(logits.shape, F32)
    for kk in range(TOP_K):
        route = jnp.where(lane == kk, ids[kk].astype(F32), route)
        route = jnp.where(lane == TOP_K + kk, ex[kk] / den, route)
    route_ref[...] = route


def _mix(z, attn, ga, gb, x, wts, alpha, n_exp, tm):
    t, d = x.shape
    tm = min(tm, t)
    wv, wg, wup, wo, g1, b1, wr, br = wts

    def rows(w):
        return pl.BlockSpec((tm, w), lambda i: (i, 0))

    def whole(a):
        return pl.BlockSpec(a.shape, lambda i: (0,) * a.ndim, pipeline_mode=pl.Buffered(1))

    return pl.pallas_call(
        functools.partial(_mix_kernel, alpha=alpha, n_exp=n_exp),
        grid=(t // tm,),
        in_specs=[rows(z.shape[1]), rows(attn.shape[1]), rows(d), rows(d), rows(d)] + [whole(w) for w in wts],
        out_specs=[rows(d), rows(LANES)],
        out_shape=[jax.ShapeDtypeStruct((t, d), F32), jax.ShapeDtypeStruct((t, LANES), F32)],
        compiler_params=_cparams(1),
        name="mix_ln1_router",
    )(z, attn, ga, gb, x, *wts)


def _rank_kernel(r_ref, rank_ref, cnt_ref, carry):
    @pl.when(pl.program_id(0) == 0)
    def _():
        carry[...] = jnp.zeros_like(carry)

    r = r_ref[...]
    tm = r.shape[0]
    lane = lax.broadcasted_iota(I32, r.shape, 1)
    ids = [r[:, kk:kk + 1].astype(I32) for kk in range(TOP_K)]
    onehot = jnp.zeros(r.shape, F32)
    for e in ids:
        onehot = onehot + (lane == e).astype(F32)
    tri = (lax.broadcasted_iota(I32, (tm, tm), 0) > lax.broadcasted_iota(I32, (tm, tm), 1)).astype(BF16)
    before = jnp.dot(tri, onehot.astype(BF16), preferred_element_type=F32) + carry[...]
    out = jnp.zeros(r.shape, F32)
    for kk, e in enumerate(ids):
        rk = jnp.sum(jnp.where(lane == e, before, 0.0), axis=1, keepdims=True)
        out = jnp.where(lane == kk, rk, out)
    rank_ref[...] = out.astype(I32)
    carry[...] = carry[...] + jnp.sum(onehot, axis=0, keepdims=True)
    cnt_ref[...] = carry[...]


def _ranks(route_all, tm):
    t = route_all.shape[0]
    return pl.pallas_call(
        _rank_kernel,
        grid=(t // tm,),
        in_specs=[pl.BlockSpec((tm, LANES), lambda i: (i, 0))],
        out_specs=[pl.BlockSpec((tm, LANES), lambda i: (i, 0)), pl.BlockSpec((1, LANES), lambda i: (0, 0))],
        out_shape=[jax.ShapeDtypeStruct((t, LANES), I32), jax.ShapeDtypeStruct((1, LANES), F32)],
        scratch_shapes=[pltpu.VMEM((1, LANES), F32)],
        compiler_params=_cparams(1),
        name="moe_rank",
    )(route_all)


def _dispatch_kernel(nu_ref, dest1_ref, dest2_ref, pad_ref, h1_ref, h2_ref, xs_ref, zero_ref, sem, *, n1, npad):
    i = pl.program_id(0)

    def scatter(h_ref, dest_ref):
        def row_copy(r, d):
            return pltpu.make_async_copy(h_ref.at[pl.ds(r, 1), :], xs_ref.at[pl.ds(d, 1), :], sem)

        def issue(r, _):
            for kk in range(TOP_K):
                row_copy(r, dest_ref[r * TOP_K + kk]).start(priority=kk % DMA_PRIORITIES)
            return 0

        def drain(r, _):
            for _kk in range(TOP_K):
                row_copy(0, 0).wait()
            return 0

        lax.fori_loop(0, h_ref.shape[0], issue, 0)
        lax.fori_loop(0, h_ref.shape[0], drain, 0)

    @pl.when(i == 0)
    def _():
        zero_ref[...] = jnp.zeros_like(zero_ref)

    @pl.when(i < n1)
    def _():
        scatter(h1_ref, dest1_ref)

    @pl.when(i == n1)
    def _():
        scatter(h2_ref, dest2_ref)

    @pl.when((i > n1) & (i <= n1 + npad))
    def _():
        def zero_row(d):
            return pltpu.make_async_copy(zero_ref.at[pl.ds(0, 1), :], xs_ref.at[pl.ds(d, 1), :], sem)

        def issue(j, _):
            d = pad_ref[j]

            @pl.when(d >= 0)
            def _():
                zero_row(d).start()
            return 0

        def drain(j, _):
            @pl.when(pad_ref[j] >= 0)
            def _():
                zero_row(0).wait()
            return 0

        lax.fori_loop(0, pad_ref.shape[0], issue, 0)
        lax.fori_loop(0, pad_ref.shape[0], drain, 0)

    @pl.when(i == n1 + npad + 1)
    def _():
        nblk = xs_ref.shape[0] // MOE_BLOCK

        def zero_block(j):
            r0 = pl.multiple_of(j * MOE_BLOCK, MOE_BLOCK)
            return pltpu.make_async_copy(zero_ref, xs_ref.at[pl.ds(r0, MOE_BLOCK), :], sem)

        def issue(j, _):
            zero_block(j).start()
            return 0

        def drain(j, _):
            zero_block(j).wait()
            return 0

        lax.fori_loop(nu_ref[0], nblk, issue, 0)
        lax.fori_loop(nu_ref[0], nblk, drain, 0)


def _dispatch(h1, h2, dest1, dest2, pad_rows, n_used, n_rows, tm):
    t1, d = h1.shape
    n1 = t1 // tm
    per_step = tm * TOP_K
    npad = pad_rows.shape[0] // per_step
    assert t1 % tm == 0 and pad_rows.shape[0] % per_step == 0
    smem = pltpu.SMEM
    return pl.pallas_call(
        functools.partial(_dispatch_kernel, n1=n1, npad=npad),
        grid=(n1 + npad + 2,),
        in_specs=[pl.BlockSpec((1,), lambda i: (0,), memory_space=smem),
                  pl.BlockSpec((per_step,), lambda i: (jnp.minimum(i, n1 - 1),), memory_space=smem),
                  pl.BlockSpec(dest2.shape, lambda i: (0,), memory_space=smem),
                  pl.BlockSpec((per_step,), lambda i: (jnp.clip(i - n1 - 1, 0, npad - 1),), memory_space=smem),
                  pl.BlockSpec((tm, d), lambda i: (jnp.minimum(i, n1 - 1), 0)),
                  pl.BlockSpec(h2.shape, lambda i: (0, 0))],
        out_specs=pl.BlockSpec(memory_space=pl.ANY),
        out_shape=jax.ShapeDtypeStruct((n_rows, d), F32),
        scratch_shapes=[pltpu.VMEM((MOE_BLOCK, d), F32), pltpu.SemaphoreType.DMA(())],
        compiler_params=_cparams(1),
        name="moe_dispatch",
    )(n_used, dest1, dest2, pad_rows, h1, h2)


def _expert_weights(be_ref, nu, i, f, nf, fetch, land, wb):
    e = be_ref[i]
    first = (i == 0) | (e != be_ref[jnp.maximum(i - 1, 0)])

    @pl.when(first)
    def _():
        @pl.when((i == 0) & (f == 0))
        def _():
            for cp in fetch(e, f):
                cp.start()

        for cp in fetch(e, f):
            cp.wait()
        for m in range(land.shape[0]):
            wb[m] = land[m].astype(BF16)

        last = be_ref.shape[0] - 1
        nxt = lax.while_loop(lambda j: (j < nu) & (be_ref[jnp.minimum(j, last)] == e), lambda j: j + 1, i + 1)

        @pl.when(nxt < nu)
        def _():
            for cp in fetch(be_ref[jnp.minimum(nxt, last)], f):
                cp.start()

        @pl.when((nxt >= nu) & (f + 1 < nf))
        def _():
            for cp in fetch(be_ref[0], f + 1):
                cp.start()


def _g1_kernel(be_ref, nu_ref, xs_ref, bg_ref, bu_ref, wg_hbm, wu_hbm, act_ref, land, wb, sem):
    f = pl.program_id(0)
    i = pl.program_id(1)
    nu = nu_ref[0]
    tf = land.shape[2]

    def fetch(e, ff):
        c0 = pl.multiple_of(ff * tf, tf)
        return (pltpu.make_async_copy(wg_hbm.at[e, :, pl.ds(c0, tf)], land.at[0], sem.at[0]),
                pltpu.make_async_copy(wu_hbm.at[e, :, pl.ds(c0, tf)], land.at[1], sem.at[1]))

    @pl.when(i < nu)
    def _():
        _expert_weights(be_ref, nu, i, f, pl.num_programs(0), fetch, land, wb)
        x = xs_ref[...].astype(BF16)
        g = jnp.dot(x, wb[0], preferred_element_type=F32) + bg_ref[0]
        up = jnp.dot(x, wb[1], preferred_element_type=F32) + bu_ref[0]
        g = jnp.minimum(g, SWIGLU_LIMIT)
        up = jnp.clip(up, -SWIGLU_LIMIT, SWIGLU_LIMIT)
        act_ref[...] = ((up + 1.0) * (g * _sigmoid(SWIGLU_ALPHA * g))).astype(act_ref.dtype)

    @pl.when(i >= nu)
    def _():
        act_ref[...] = jnp.zeros_like(act_ref)


def _g2_kernel(be_ref, nu_ref, act_ref, bd_ref, wd_hbm, y_ref, land, wb, sem):
    i = pl.program_id(0)
    nu = nu_ref[0]

    def fetch(e, ff):
        del ff
        return (pltpu.make_async_copy(wd_hbm.at[e], land.at[0], sem.at[0]),)

    @pl.when(i < nu)
    def _():
        _expert_weights(be_ref, nu, i, 0, 1, fetch, land, wb)
        y_ref[...] = jnp.dot(act_ref[...], wb[0], preferred_element_type=F32) + bd_ref[0]

    @pl.when(i >= nu)
    def _():
        y_ref[...] = jnp.zeros_like(y_ref)


def _experts(xs, blk_e, n_used, w_gate, b_gate, w_up, b_up, w_down, b_down):
    n_rows, d = xs.shape
    n_exp, _, dff = w_gate.shape
    nblk = n_rows // MOE_BLOCK
    tf = min(1024, dff)
    nf = dff // tf

    def blk_row(f, i, be, nu):
        return (jnp.minimum(i, nu[0] - 1), 0)

    def b_spec(f, i, be, nu):
        return (be[jnp.minimum(i, nu[0] - 1)], 0, f)

    hbm = pl.BlockSpec(memory_space=pl.ANY)
    act = pl.pallas_call(
        _g1_kernel,
        grid_spec=pltpu.PrefetchScalarGridSpec(
            num_scalar_prefetch=2, grid=(nf, nblk),
            in_specs=[pl.BlockSpec((MOE_BLOCK, d), blk_row),
                      pl.BlockSpec((1, 1, tf), b_spec), pl.BlockSpec((1, 1, tf), b_spec), hbm, hbm],
            out_specs=pl.BlockSpec((MOE_BLOCK, tf), lambda f, i, be, nu: (i, f)),
            scratch_shapes=[pltpu.VMEM((2, d, tf), F32), pltpu.VMEM((2, d, tf), BF16),
                            pltpu.SemaphoreType.DMA((2,))]),
        out_shape=jax.ShapeDtypeStruct((n_rows, dff), BF16),
        compiler_params=_cparams(2),
        name="moe_gate_up",
    )(blk_e, n_used, xs, b_gate.reshape(n_exp, 1, dff), b_up.reshape(n_exp, 1, dff), w_gate, w_up)

    return pl.pallas_call(
        _g2_kernel,
        grid_spec=pltpu.PrefetchScalarGridSpec(
            num_scalar_prefetch=2, grid=(nblk,),
            in_specs=[pl.BlockSpec((MOE_BLOCK, dff), lambda i, be, nu: (jnp.minimum(i, nu[0] - 1), 0)),
                      pl.BlockSpec((1, 1, d), lambda i, be, nu: (be[jnp.minimum(i, nu[0] - 1)], 0, 0)), hbm],
            out_specs=pl.BlockSpec((MOE_BLOCK, d), lambda i, be, nu: (i, 0)),
            scratch_shapes=[pltpu.VMEM((1, dff, d), F32), pltpu.VMEM((1, dff, d), BF16),
                            pltpu.SemaphoreType.DMA((1,))]),
        out_shape=jax.ShapeDtypeStruct((n_rows, d), F32),
        compiler_params=_cparams(1),
        name="moe_down",
    )(blk_e, n_used, act, b_down.reshape(n_exp, 1, d), w_down)


def _combine_kernel(dest_ref, dest_next_ref, h_ref, route_ref, g_ref, b_ref, yb_ref, o_ref, buf, sem, *, alpha):
    tm = h_ref.shape[0]
    i = pl.program_id(0)
    slot = lax.rem(i, 2)

    def row_copy(r, kk, d, s):
        return pltpu.make_async_copy(yb_ref.at[pl.ds(d, 1), :], buf.at[s, kk, pl.ds(r, 1), :], sem.at[s])

    def issue(idx_ref, s):
        def body(r, _):
            for kk in range(TOP_K):
                row_copy(r, kk, idx_ref[r * TOP_K + kk], s).start(priority=kk % DMA_PRIORITIES)
            return 0
        lax.fori_loop(0, tm, body, 0)

    @pl.when(i == 0)
    def _():
        issue(dest_ref, 0)

    @pl.when(i + 1 < pl.num_programs(0))
    def _():
        issue(dest_next_ref, 1 - slot)

    def drain(r, _):
        for kk in range(TOP_K):
            row_copy(0, kk, 0, slot).wait()
        return 0

    lax.fori_loop(0, tm, drain, 0)
    route = route_ref[...]
    acc = alpha * h_ref[...]
    for kk in range(TOP_K):
        acc = acc + route[:, TOP_K + kk:TOP_K + kk + 1] * buf[slot, kk]
    o_ref[...] = _layernorm(acc, g_ref[...], b_ref[...])


def _combine(h, route, dest, yb, g2, b2, alpha, tm):
    t, d = h.shape
    tm = min(tm, t)
    nt = t // tm
    return pl.pallas_call(
        functools.partial(_combine_kernel, alpha=alpha),
        grid=(nt,),
        in_specs=[pl.BlockSpec((tm * TOP_K,), lambda i: (i,), memory_space=pltpu.SMEM),
                  pl.BlockSpec((tm * TOP_K,), lambda i: (jnp.minimum(i + 1, nt - 1),), memory_space=pltpu.SMEM),
                  pl.BlockSpec((tm, d), lambda i: (i, 0)),
                  pl.BlockSpec((tm, LANES), lambda i: (i, 0)),
                  pl.BlockSpec((1, d), lambda i: (0, 0)),
                  pl.BlockSpec((1, d), lambda i: (0, 0)),
                  pl.BlockSpec(memory_space=pl.ANY)],
        out_specs=pl.BlockSpec((tm, d), lambda i: (i, 0)),
        out_shape=jax.ShapeDtypeStruct((t, d), F32),
        scratch_shapes=[pltpu.VMEM((2, TOP_K, tm, d), F32), pltpu.SemaphoreType.DMA((2,))],
        compiler_params=_cparams(1),
        name="moe_combine_ln2",
    )(dest, dest, h, route, g2, b2, yb)


def _layer(x_prompt, x_sample, cache_k, cache_v, page_table, st_re, st_im, w_in,
           lam_re, lam_im, log_dt, b_re, b_im, c_re, c_im, d_skip,
           w_glu_val, w_glu_gate, w_attn_up, w_o, ln1_g, ln1_b,
           w_router, b_router, w_gate, b_gate, w_up, b_up, w_down, b_down, ln2_g, ln2_b, depth):
    n_seq, s_len, d_model = x_prompt.shape
    db, dec_seq, _ = x_sample.shape
    assert dec_seq == 1, "decode path handles one new token per sequence"
    n_phys, page, n_heads, hd = cache_k.shape
    d_attn = n_heads * hd
    n_groups, n_state = st_re.shape[1:]
    d_ssm = n_groups * b_re.shape[-1]
    n_exp = w_router.shape[1]
    alpha = (2 * depth) ** 0.25
    tp, ts = n_seq * s_len, db
    past = page_table.shape[1] * page
    col_q, col_k, col_v, col_u = 0, d_attn, 2 * d_attn, 3 * d_attn
    col_ga, col_gb = 3 * d_attn + d_ssm, 3 * d_attn + d_ssm + d_model

    wb = w_in.astype(BF16)
    s5p = _s5_discretise(lam_re.astype(F32), lam_im.astype(F32), log_dt, b_re.astype(F32), b_im.astype(F32))
    s5p = s5p + (c_re.astype(F32), c_im.astype(F32), d_skip)
    wr_pad = jnp.zeros((d_model, LANES), BF16).at[:, :n_exp].set(w_router.astype(BF16))
    br_pad = jnp.zeros((1, LANES), F32).at[0, :n_exp].set(b_router.astype(F32))
    mix_w = (w_glu_val.astype(BF16), w_glu_gate.astype(BF16), w_attn_up.astype(BF16), w_o.astype(BF16),
             ln1_g.reshape(1, d_model).astype(F32), ln1_b.reshape(1, d_model).astype(F32), wr_pad, br_pad)

    def project(x2, pos, reps, tm, tag):
        xb = x2.astype(BF16)
        tabs = tuple(jnp.tile(tab, (reps, 1)) for tab in _rope_tables(pos, hd))
        kw = dict(tm=tm, tn=1024)
        q = _proj(xb, wb, col_q, d_attn, BF16 if tag == "p" else F32, rope_tabs=tabs, hd=hd, name=f"proj_q_{tag}", **kw)
        k = _proj(xb, wb, col_k, d_attn, F32, rope_tabs=tabs, hd=hd, name=f"proj_k_{tag}", **kw)
        v = _proj(xb, wb, col_v, d_attn, F32, name=f"proj_v_{tag}", **kw)
        u = _proj(xb, wb, col_u, d_ssm, F32, name=f"proj_u_{tag}", **kw)
        ga = _proj(xb, wb, col_ga, d_model, F32, name=f"proj_ga_{tag}", **kw)
        gb = _proj(xb, wb, col_gb, d_model, F32, name=f"proj_gb_{tag}", **kw)
        return q, k, v, u, ga, gb

    xp2 = x_prompt.reshape(tp, d_model)
    q1, k1, v1, u1, ga1, gb1 = project(xp2, jnp.arange(s_len), n_seq, 1024, "p")
    attn1, ksum = _moba_prefill(q1, k1, v1, n_seq, s_len, n_heads, hd, page_table, cache_k)
    z1, hr1, hi1 = _s5_prompt(u1.reshape(n_seq, s_len, d_ssm), s5p, n_seq, s_len)
    h1, route1 = _mix(z1.reshape(tp, d_ssm), attn1, ga1, gb1, xp2, mix_w, alpha, n_exp, 256)

    xs2 = x_sample.reshape(ts, d_model)
    q2, k2, v2, u2, ga2, gb2 = project(xs2, jnp.full((1,), past, I32), ts, ts, "s")
    attn2 = _moba_decode(q2.reshape(ts, n_heads, hd), k2.reshape(ts, n_heads, hd), v2.reshape(ts, n_heads, hd),
                         cache_k, cache_v, page_table, ksum)
    z2, hr2, hi2 = _s5_sample(u2, st_re, st_im, s5p)
    h2, route2 = _mix(z2, attn2.reshape(ts, d_attn), ga2, gb2, xs2, mix_w, alpha, n_exp, ts)

    tr = 256
    t_all = tp + ts
    t_pad = -(-t_all // tr) * tr
    route_all = jnp.concatenate([route1, route2, jnp.full((t_pad - t_all, LANES), -1.0, F32)], axis=0)
    rank_all, cnt = _ranks(route_all, tr)
    counts = cnt[0, :n_exp].astype(I32)
    padded = (counts + MOE_BLOCK - 1) // MOE_BLOCK * MOE_BLOCK
    pad_end = jnp.cumsum(padded)
    pad_start = pad_end - padded
    ids = route_all[:t_all, :TOP_K].astype(I32)
    dest = (pad_start[ids] + rank_all[:t_all, :TOP_K]).astype(I32)
    nblk = -(-(t_all * TOP_K) // MOE_BLOCK) + n_exp
    blk_first_row = jnp.arange(nblk, dtype=I32) * MOE_BLOCK
    blk_e = jnp.minimum(jnp.sum(pad_end[None, :] <= blk_first_row[:, None], axis=1), n_exp - 1).astype(I32)
    n_used = (pad_end[-1:] // MOE_BLOCK).astype(I32)
    dest1 = dest[:tp].reshape(-1)
    dest2 = dest[tp:].reshape(-1)

    cand = (pad_start + counts)[:, None] + jnp.arange(MOE_BLOCK, dtype=I32)[None, :]
    pad_rows = jnp.where(cand < pad_end[:, None], cand, -1).reshape(-1).astype(I32)
    per_step = 256 * TOP_K
    pad_rows = jnp.concatenate([pad_rows, jnp.full((-pad_rows.shape[0] % per_step,), -1, I32)])
    xs = _dispatch(h1, h2, dest1, dest2, pad_rows, n_used, nblk * MOE_BLOCK, 256)
    yb = _experts(xs, blk_e, n_used, w_gate, b_gate, w_up, b_up, w_down, b_down)
    g2 = ln2_g.reshape(1, d_model).astype(F32)
    b2 = ln2_b.reshape(1, d_model).astype(F32)
    y1 = _combine(h1, route1, dest1, yb, g2, b2, alpha, 256)
    y2 = _combine(h2, route2, dest2, yb, g2, b2, alpha, ts)

    return (y1.reshape(n_seq, s_len, d_model), y2.reshape(db, 1, d_model),
            k1.reshape(n_seq, s_len, n_heads, hd), v1.reshape(n_seq, s_len, n_heads, hd), hr1, hi1,
            k2.reshape(db, 1, n_heads, hd), v2.reshape(db, 1, n_heads, hd), hr2, hi2)


def kernel(x_prompt, x_sample, cache_k, cache_v, page_table, state_ssm_re, state_ssm_im, w_in, ssm_lambda_re, ssm_lambda_im, ssm_log_dt, ssm_b_re, ssm_b_im, ssm_c_re, ssm_c_im, ssm_d, w_glu_val, w_glu_gate, w_attn_up, w_o, ln1_g, ln1_b, w_router, b_router, w_gate, b_gate, w_up, b_up, w_down, b_down, ln2_g, ln2_b):
    depth = w_in.shape[0]
    assert depth == 1, "single-layer step"
    outs = _layer(x_prompt, x_sample, cache_k[0], cache_v[0], page_table, state_ssm_re[0], state_ssm_im[0],
                  w_in[0], ssm_lambda_re[0], ssm_lambda_im[0], ssm_log_dt[0], ssm_b_re[0], ssm_b_im[0],
                  ssm_c_re[0], ssm_c_im[0], ssm_d[0], w_glu_val[0], w_glu_gate[0], w_attn_up[0], w_o[0],
                  ln1_g[0], ln1_b[0], w_router[0], b_router[0], w_gate[0], b_gate[0], w_up[0], b_up[0],
                  w_down[0], b_down[0], ln2_g[0], ln2_b[0], depth)
    y1, y2, k1, v1, hr1, hi1, k2, v2, hr2, hi2 = outs
    return (y1, y2, k1[None], v1[None], hr1[None], hi1[None], k2[None], v2[None], hr2[None], hi2[None])
```

```python
import functools
import math

import jax
import jax.numpy as jnp
from jax import lax
from jax.experimental import pallas as pl
from jax.experimental.pallas import tpu as pltpu

F32, BF16, I32 = jnp.float32, jnp.bfloat16, jnp.int32

ROPE_THETA = 500000.0
ROT_FRACTION = 4
MOBA_BLOCK = 256
MOBA_TOPK = 3
TOP_K = 4
SWIGLU_LIMIT = 7.0
SWIGLU_ALPHA = 1.702
LN_EPS = 1e-5
MOE_BLOCK = 256

LANES = 128
SUBLANES = 8
VMEM_LIMIT_BYTES = 56 * 1024 * 1024
DMA_PRIORITIES = 2

NEG_BIG = -1e30


def _cparams(n_axes):
    return pltpu.CompilerParams(dimension_semantics=("arbitrary",) * n_axes,
                                vmem_limit_bytes=VMEM_LIMIT_BYTES)


def _sigmoid(x):
    return 1.0 / (1.0 + jnp.exp(-x))


def _gelu_tanh(x):
    c = math.sqrt(2.0 / math.pi)
    return x * (0.5 * (1.0 + jnp.tanh(c * (x + 0.044715 * (x * x * x)))))


def _layernorm(r, g, b):
    mu = jnp.mean(r, axis=-1, keepdims=True)
    xc = r - mu
    var = jnp.mean(xc * xc, axis=-1, keepdims=True)
    return xc * lax.rsqrt(var + LN_EPS) * g + b


def _proj_kernel(x_ref, w_ref, *rest, rope, rot_half, hd):
    acc = jnp.dot(x_ref[...], w_ref[...], preferred_element_type=F32)
    if not rope:
        (o_ref,) = rest
        o_ref[...] = acc.astype(o_ref.dtype)
        return
    cos_ref, sin_ref, o_ref = rest
    cos = cos_ref[...]
    sin = sin_ref[...]
    first = lax.broadcasted_iota(I32, cos.shape, 1) < rot_half
    for hh in range(acc.shape[1] // hd):
        xh = acc[:, hh * hd:(hh + 1) * hd]
        rot = jnp.where(first, pltpu.roll(xh, hd - rot_half, 1), pltpu.roll(xh, rot_half, 1))
        o_ref[:, hh * hd:(hh + 1) * hd] = (xh * cos + rot * sin).astype(o_ref.dtype)


def _proj(xb, wb, col0, ncols, out_dtype, *, tm, tn, rope_tabs=None, hd=LANES, name):
    t, k = xb.shape
    tm = min(tm, t)
    tn = min(tn, ncols)
    off = col0 // tn
    assert col0 % tn == 0 and ncols % tn == 0 and t % tm == 0
    in_specs = [pl.BlockSpec((tm, k), lambda i, j: (i, 0)),
                pl.BlockSpec((k, tn), lambda i, j: (0, j + off))]
    args = [xb, wb]
    if rope_tabs is not None:
        in_specs += [pl.BlockSpec((tm, hd), lambda i, j: (i, 0))] * 2
        args += list(rope_tabs)
    return pl.pallas_call(
        functools.partial(_proj_kernel, rope=rope_tabs is not None, rot_half=hd // ROT_FRACTION // 2, hd=hd),
        grid=(t // tm, ncols // tn),
        in_specs=in_specs,
        out_specs=pl.BlockSpec((tm, tn), lambda i, j: (i, j)),
        out_shape=jax.ShapeDtypeStruct((t, ncols), out_dtype),
        compiler_params=_cparams(2),
        name=name,
    )(*args)


def _rope_tables(pos, hd):
    rot = hd // ROT_FRACTION
    half = rot // 2
    inv = ROPE_THETA ** (-jnp.arange(half, dtype=F32) * 2.0 / rot)
    ang = pos.astype(F32)[:, None] * inv[None, :]
    cos, sin = jnp.cos(ang), jnp.sin(ang)
    t = pos.shape[0]
    cosf = jnp.concatenate([cos, cos, jnp.ones((t, hd - rot), F32)], axis=1)
    sinf = jnp.concatenate([-sin, sin, jnp.zeros((t, hd - rot), F32)], axis=1)
    return cosf, sinf


def _page_sums(page_refs, o_ref, ppb):
    page = page_refs[0].shape[1]
    chains = 16 if page % 16 == 0 else 1
    for r in range(len(page_refs) // ppb):
        x = None
        for u in range(ppb):
            ref = page_refs[r * ppb + u]
            for c in range(page // chains):
                piece = ref[0, c * chains:(c + 1) * chains]
                x = piece if x is None else x + piece
        o_ref[0, r] = jnp.sum(x, axis=0)


def _attn_kernel(pt_ref, q_ref, k_ref, v_ref, *rest, blk, nb, grp, scale, npg, ppb, ksteps):
    del pt_ref
    page_refs = rest[:npg]
    if npg:
        o_ref, ks_ref, ka_ref, vb_ref, km_ref, s_ref = rest[npg:]
        step = (pl.program_id(0) * pl.num_programs(1) + pl.program_id(1)) * pl.num_programs(2) + pl.program_id(2)

        @pl.when(step < ksteps)
        def _():
            _page_sums(page_refs, ks_ref, ppb)
    else:
        o_ref, ka_ref, vb_ref, km_ref, s_ref = rest

    qi = pl.program_id(2)
    s_len, hd = k_ref.shape

    @pl.when(qi == 0)
    def _():
        vb_ref[...] = v_ref[...].astype(BF16)
        ka_ref[:, :hd] = k_ref[...].astype(BF16)
        row = lax.broadcasted_iota(I32, (s_len, hd), 0)
        lane = lax.broadcasted_iota(I32, (s_len, hd), 1)
        member = (row >= lane * blk) & (row < (lane + 1) * blk)
        ka_ref[:, hd:] = jnp.where(member, 1.0, 0.0).astype(BF16)
        km_ref[...] = jnp.zeros_like(km_ref)
        for j in range(nb):
            km_ref[j:j + 1, :] = jnp.mean(k_ref[j * blk:(j + 1) * blk, :], axis=0, keepdims=True)

    q = q_ref[...]
    tq = q.shape[0]
    nt = (((1,), (1,)), ((), ()))
    nbp = -(-nb // SUBLANES) * SUBLANES
    gate_t = lax.dot_general(km_ref[:nbp, :], q.astype(F32), nt,
                             precision=lax.Precision.HIGHEST, preferred_element_type=F32)
    blk_id = lax.broadcasted_iota(I32, (nbp, tq), 0)
    g = jnp.where(blk_id < qi, gate_t, -jnp.inf)
    sel = jnp.zeros(g.shape, jnp.bool_)
    for _ in range(MOBA_TOPK):
        m = jnp.max(g, axis=0, keepdims=True)
        idx = jnp.min(jnp.where(g == m, blk_id, nbp), axis=0, keepdims=True)
        hit = blk_id == idx
        sel = sel | (hit & (m > -jnp.inf))
        g = jnp.where(hit, -jnp.inf, g)
    bias_t = jnp.concatenate([jnp.where(sel, 0.0, NEG_BIG), jnp.full((hd - nbp, tq), NEG_BIG, F32)], axis=0)
    qa = jnp.concatenate([q, bias_t.T.astype(BF16)], axis=1)
    gw = grp * blk
    ntile = blk // LANES

    def lane_fold(x, acc, op):
        for c in range(x.shape[1] // LANES):
            acc = op(acc, x[:, c * LANES:(c + 1) * LANES])
        return acc

    d0 = pl.multiple_of(qi * blk, blk)
    s_own = lax.dot_general(q, ka_ref[pl.ds(d0, blk), :hd], nt, preferred_element_type=F32) * scale
    row = lax.broadcasted_iota(I32, (tq, blk), 0)
    col = lax.broadcasted_iota(I32, (tq, blk), 1)
    s_own = jnp.where(col <= row, s_own, -jnp.inf)
    m_own = lane_fold(s_own[:, LANES:], s_own[:, :LANES], jnp.maximum) if ntile > 1 else s_own

    def attend(ng):
        m_part = m_own
        for gi in range(ng):
            s = lax.dot_general(qa, ka_ref[gi * gw:(gi + 1) * gw, :], nt, preferred_element_type=F32) * scale
            s_ref[gi] = s
            m_part = lane_fold(s, m_part, jnp.maximum)
        m = jnp.max(m_part, axis=1, keepdims=True)
        p_own = jnp.exp(s_own - m)
        l_part = lane_fold(p_own[:, LANES:], p_own[:, :LANES], jnp.add) if ntile > 1 else p_own
        acc = jnp.dot(p_own.astype(BF16), vb_ref[pl.ds(d0, blk), :], preferred_element_type=F32)
        for gi in range(ng):
            p = jnp.exp(s_ref[gi] - m)
            acc = acc + jnp.dot(p.astype(BF16), vb_ref[gi * gw:(gi + 1) * gw, :], preferred_element_type=F32)
            l_part = lane_fold(p, l_part, jnp.add)
        o_ref[...] = (acc / jnp.sum(l_part, axis=1, keepdims=True)).astype(o_ref.dtype)

    ng = lax.div(qi + (grp - 1), grp)
    for n_groups in range((nb - 1 + grp - 1) // grp + 1):
        pl.when(ng == n_groups)(functools.partial(attend, n_groups))


def _ksum_plan(page_table, cache_k):
    db, n_pages = page_table.shape
    page = cache_k.shape[1]
    ppb = MOBA_BLOCK // page
    assert MOBA_BLOCK % page == 0 and n_pages % ppb == 0 and n_pages >= ppb
    npg = 8 if n_pages % 8 == 0 else ppb
    return db, n_pages // ppb, ppb, npg, n_pages // npg


def _moba_prefill(q, k, v, n_seq, s_len, n_heads, hd, page_table, cache_k):
    blk = MOBA_BLOCK
    nb = s_len // blk
    assert s_len % blk == 0 and nb <= hd
    t = n_seq * s_len
    grp = 4 if nb % 4 == 0 else 1
    db, nbp, ppb, npg, nstep = _ksum_plan(page_table, cache_k)
    ksteps = db * nstep
    fuse = ksteps <= n_seq * n_heads * nb
    if not fuse:
        npg = 0

    def ks_step(n, h, i):
        return jnp.minimum((n * n_heads + h) * nb + i, ksteps - 1)

    def page_spec(r):
        return pl.BlockSpec((1,) + cache_k.shape[1:], lambda n, h, i, pg: (pg[ks_step(n, h, i), r], 0, 0, 0))

    out_specs = [pl.BlockSpec((blk, hd), lambda n, h, i, pg: (n * nb + i, h))]
    out_shape = [jax.ShapeDtypeStruct((t, n_heads * hd), BF16)]
    if fuse:
        out_specs.append(pl.BlockSpec((1, npg // ppb) + cache_k.shape[2:],
                                      lambda n, h, i, pg: (ks_step(n, h, i), 0, 0, 0)))
        out_shape.append(jax.ShapeDtypeStruct((ksteps, npg // ppb) + cache_k.shape[2:], F32))
    outs = pl.pallas_call(
        functools.partial(_attn_kernel, blk=blk, nb=nb, grp=grp, scale=1.0 / math.sqrt(hd),
                          npg=npg, ppb=ppb, ksteps=ksteps),
        grid_spec=pltpu.PrefetchScalarGridSpec(
            num_scalar_prefetch=1, grid=(n_seq, n_heads, nb),
            in_specs=[pl.BlockSpec((blk, hd), lambda n, h, i, pt: (n * nb + i, h)),
                      pl.BlockSpec((s_len, hd), lambda n, h, i, pt: (n, h)),
                      pl.BlockSpec((s_len, hd), lambda n, h, i, pt: (n, h))] + [page_spec(r) for r in range(npg)],
            out_specs=out_specs,
            scratch_shapes=[pltpu.VMEM((s_len, 2 * hd), BF16), pltpu.VMEM((s_len, hd), BF16),
                            pltpu.VMEM((hd, hd), F32), pltpu.VMEM((nb // grp, blk, grp * blk), F32)]),
        out_shape=out_shape,
        compiler_params=_cparams(3),
        name="moba_prefill",
    )(page_table.reshape(ksteps, -1), q, k, v, *([cache_k] * npg))
    return (outs[0], outs[1].reshape((db, nbp) + cache_k.shape[2:])) if fuse else (outs[0], None)


def _dec_ksum_kernel(pt_ref, *refs, npg, ppb):
    del pt_ref
    _page_sums(refs[:npg], refs[npg], ppb)


def _dec_sel_kernel(ks_ref, q_ref, o_ref, *, blk):
    km = ks_ref[0] * (1.0 / blk)
    nbp, n_heads, _ = km.shape
    g = jnp.sum(km * q_ref[...], axis=2, keepdims=True)
    blk_id = lax.broadcasted_iota(I32, g.shape, 0)
    lane = lax.broadcasted_iota(I32, (n_heads, LANES), 1)
    out = jnp.full((n_heads, LANES), nbp, I32)
    for r in range(MOBA_TOPK):
        m = jnp.max(g, axis=0, keepdims=True)
        idx = jnp.min(jnp.where(g == m, blk_id, nbp), axis=0, keepdims=True)
        pick = jnp.where(m > -jnp.inf, idx, nbp)[0]
        out = jnp.where(lane == r, pick, out)
        g = jnp.where(blk_id == idx, -jnp.inf, g)
    o_ref[0] = out


def _dec_attn_kernel(pt_ref, sel_ref, q_ref, kn_ref, vn_ref, ck_ref, cv_ref, o_ref, kbuf, vbuf, sem,
                     *, nsel, ppb, nbp, scale):
    b = pl.program_id(0)
    nb = pl.num_programs(0)
    n_heads = q_ref.shape[1]
    npg = nsel * ppb

    def copies(bb, slot):
        out = []
        for h in range(n_heads):
            for r in range(nsel):
                blk_id = jnp.minimum(sel_ref[(bb * n_heads + h) * nsel + r], nbp - 1)
                for u in range(ppb):
                    phys = pt_ref[bb, blk_id * ppb + u]
                    j = h * npg + r * ppb + u
                    out.append(pltpu.make_async_copy(ck_ref.at[phys, :, h, :], kbuf.at[slot, j], sem.at[slot]))
                    out.append(pltpu.make_async_copy(cv_ref.at[phys, :, h, :], vbuf.at[slot, j], sem.at[slot]))
        return out

    slot = lax.rem(b, 2)

    @pl.when(b == 0)
    def _():
        for cp in copies(b, 0):
            cp.start()

    @pl.when(b + 1 < nb)
    def _():
        for cp in copies(b + 1, 1 - slot):
            cp.start()

    for cp in copies(b, slot):
        cp.wait()

    q8 = q_ref[0]
    page = kbuf.shape[2]
    for h in range(n_heads):
        ks = jnp.concatenate([kbuf[slot, h * npg + j] for j in range(npg)], axis=0)
        vs = jnp.concatenate([vbuf[slot, h * npg + j] for j in range(npg)], axis=0)
        s_all = lax.dot_general(q8.astype(BF16), ks.astype(BF16), (((1,), (1,)), ((), ())),
                                preferred_element_type=F32)
        s = s_all[h:h + 1] * scale
        col = lax.broadcasted_iota(I32, s.shape, 1)
        ok = jnp.zeros(s.shape, jnp.bool_)
        for r in range(nsel):
            in_seg = (col >= r * ppb * page) & (col < (r + 1) * ppb * page)
            ok = ok | (in_seg & (sel_ref[(b * n_heads + h) * nsel + r] < nbp))
        s = jnp.where(ok, s, -jnp.inf)
        qh, knh, vnh = q8[h:h + 1], kn_ref[0, h:h + 1], vn_ref[0, h:h + 1]
        s_new = jnp.sum(qh * knh, axis=1, keepdims=True) * scale
        m = jnp.maximum(jnp.max(s, axis=1, keepdims=True), s_new)
        p = jnp.exp(s - m)
        p_new = jnp.exp(s_new - m)
        den = jnp.sum(p, axis=1, keepdims=True) + p_new
        out = jnp.dot(p.astype(BF16), vs.astype(BF16), preferred_element_type=F32) + p_new * vnh
        o_ref[0, h:h + 1, :] = out / den


def _moba_decode(q, k_new, v_new, cache_k, cache_v, page_table, ksum):
    _, page, n_heads, hd = cache_k.shape
    blk = MOBA_BLOCK
    db, nbp, ppb, npg, nstep = _ksum_plan(page_table, cache_k)

    if ksum is None:
        def page_spec(r):
            return pl.BlockSpec((1, page, n_heads, hd), lambda b, s, pt: (pt[b, s * npg + r], 0, 0, 0))

        ksum = pl.pallas_call(
            functools.partial(_dec_ksum_kernel, npg=npg, ppb=ppb),
            grid_spec=pltpu.PrefetchScalarGridSpec(
                num_scalar_prefetch=1, grid=(db, nstep),
                in_specs=[page_spec(r) for r in range(npg)],
                out_specs=pl.BlockSpec((1, npg // ppb, n_heads, hd), lambda b, s, pt: (b, s, 0, 0))),
            out_shape=jax.ShapeDtypeStruct((db, nbp, n_heads, hd), F32),
            compiler_params=_cparams(2),
            name="dec_ksum",
        )(page_table, *([cache_k] * npg))

    sel = pl.pallas_call(
        functools.partial(_dec_sel_kernel, blk=blk),
        grid=(db,),
        in_specs=[pl.BlockSpec((1, nbp, n_heads, hd), lambda b: (b, 0, 0, 0)),
                  pl.BlockSpec((1, n_heads, hd), lambda b: (b, 0, 0))],
        out_specs=pl.BlockSpec((1, n_heads, LANES), lambda b: (b, 0, 0)),
        out_shape=jax.ShapeDtypeStruct((db, n_heads, LANES), I32),
        compiler_params=_cparams(1),
        name="dec_select",
    )(ksum, q)
    nsel = MOBA_TOPK
    sel = sel[:, :, :nsel].reshape(-1)

    row_spec = pl.BlockSpec((1, n_heads, hd), lambda b, pt, sl: (b, 0, 0))
    n_buf = n_heads * nsel * ppb
    return pl.pallas_call(
        functools.partial(_dec_attn_kernel, nsel=nsel, ppb=ppb, nbp=nbp, scale=1.0 / math.sqrt(hd)),
        grid_spec=pltpu.PrefetchScalarGridSpec(
            num_scalar_prefetch=2, grid=(db,),
            in_specs=[row_spec, row_spec, row_spec,
                      pl.BlockSpec(memory_space=pl.ANY), pl.BlockSpec(memory_space=pl.ANY)],
            out_specs=row_spec,
            scratch_shapes=[pltpu.VMEM((2, n_buf, page, hd), F32), pltpu.VMEM((2, n_buf, page, hd), F32),
                            pltpu.SemaphoreType.DMA((2,))]),
        out_shape=jax.ShapeDtypeStruct((db, n_heads, hd), F32),
        compiler_params=_cparams(1),
        name="dec_attend",
    )(page_table, sel, q, k_new, v_new, cache_k, cache_v)


def _s5_discretise(lam_re, lam_im, log_dt, b_re, b_im):
    dt = jnp.exp(log_dt.astype(F32))[:, None]
    mag = jnp.exp(lam_re * dt)
    ang = lam_im * dt
    a_re = mag * jnp.cos(ang)
    a_im = mag * jnp.sin(ang)
    den = lam_re * lam_re + lam_im * lam_im
    nr = a_re - 1.0
    f_re = (nr * lam_re + a_im * lam_im) / den
    f_im = (a_im * lam_re - nr * lam_im) / den
    bb_re = f_re[..., None] * b_re - f_im[..., None] * b_im
    bb_im = f_re[..., None] * b_im + f_im[..., None] * b_re
    return a_re, a_im, bb_re, bb_im


def _s5_layout(a_re, a_im, bb_re, bb_im, c_re, c_im, d_skip, n_rows):
    g, p, gc = bb_re.shape
    gb = max(1, min(g // 2, (2 * LANES) // gc, 16))
    nblk = g // gb
    eye = jnp.eye(gb, dtype=F32)

    def in_blocks(bb):
        return jnp.einsum('bgpc,gh->bgchp', bb.reshape(nblk, gb, p, gc), eye).reshape(nblk, gb * gc, gb * p)

    def out_blocks(c):
        return jnp.einsum('bgcp,gh->bgphc', c.reshape(nblk, gb, gc, p), eye).reshape(nblk, gb * p, gb * gc)

    bb = jnp.stack([in_blocks(bb_re), in_blocks(bb_im)]).astype(BF16)
    cc = jnp.concatenate([out_blocks(c_re), -out_blocks(c_im)], axis=1).astype(BF16)
    sh = g * p // 2
    a = jnp.stack([a_re.reshape(2, sh), a_im.reshape(2, sh)])
    a = jnp.repeat(a, n_rows // 2, axis=1)
    a = a.reshape(2, n_rows, sh // LANES, LANES).transpose(0, 2, 1, 3)
    return bb, cc, a, d_skip.reshape(1, g * gc).astype(F32), gb


def _s5_kernel(u_ref, bb_ref, cc_ref, a_ref, d_ref, z_ref, hre_ref, him_ref, xre, xim, hst,
               *, n_seq, length, pitch, nb_half, chains):
    c = pl.program_id(0)
    nrow = 2 * n_seq
    _, nblk, cb, sb = bb_ref.shape
    nsl = sb // LANES
    nslab = nb_half * nsl

    @pl.when(c == 0)
    def _():
        hst[...] = jnp.zeros_like(hst)

    u = u_ref[...].reshape(n_seq * length, u_ref.shape[2])
    ub = u.astype(BF16)
    for half in range(2):
        for b2 in range(nb_half):
            b = half * nb_half + b2
            lhs = ub[:, b * cb:(b + 1) * cb]
            for ri, xs in ((0, xre), (1, xim)):
                res = jnp.dot(lhs, bb_ref[ri, b], preferred_element_type=F32)
                for n in range(n_seq):
                    for s8 in range(nsl):
                        xs[b2 * nsl + s8, pl.ds((half * n_seq + n) * pitch, length), :] = (
                            res[n * length:(n + 1) * length, s8 * LANES:(s8 + 1) * LANES])

    for cg in range(nslab // chains):
        slabs = [cg * chains + kk for kk in range(chains)]
        ar = [a_ref[0, sl] for sl in slabs]
        ai = [a_ref[1, sl] for sl in slabs]
        init = (tuple(hst[0, sl] for sl in slabs), tuple(hst[1, sl] for sl in slabs))

        def step(t, carry, slabs=slabs, ar=ar, ai=ai):
            hr, hi = carry
            nr, ni = [], []
            for kk, sl in enumerate(slabs):
                rows = pl.ds(t, nrow, stride=pitch)
                xr = xre[sl, rows, :]
                xi = xim[sl, rows, :]
                r = ar[kk] * hr[kk] - ai[kk] * hi[kk] + xr
                i = ar[kk] * hi[kk] + ai[kk] * hr[kk] + xi
                xre[sl, rows, :] = r
                xim[sl, rows, :] = i
                nr.append(r)
                ni.append(i)
            return tuple(nr), tuple(ni)

        hr, hi = lax.fori_loop(0, length, step, init, unroll=2)
        for kk, sl in enumerate(slabs):
            hst[0, sl] = hr[kk]
            hst[1, sl] = hi[kk]

    for half in range(2):
        for b2 in range(nb_half):
            b = half * nb_half + b2

            def gather(xs):
                return jnp.concatenate(
                    [jnp.concatenate([xs[b2 * nsl + s8, pl.ds((half * n_seq + n) * pitch, length), :]
                                      for s8 in range(nsl)], axis=1) for n in range(n_seq)], axis=0)

            lhs = jnp.concatenate([gather(xre), gather(xim)], axis=1).astype(BF16)
            y = jnp.dot(lhs, cc_ref[b], preferred_element_type=F32)
            y = y + d_ref[:, b * cb:(b + 1) * cb] * u[:, b * cb:(b + 1) * cb]
            zz = _gelu_tanh(y).astype(z_ref.dtype)
            for n in range(n_seq):
                z_ref[n, :, b * cb:(b + 1) * cb] = zz[n * length:(n + 1) * length]

    @pl.when(c == pl.num_programs(0) - 1)
    def _():
        for sl in range(nslab):
            hre_ref[:, sl * LANES:(sl + 1) * LANES] = hst[0, sl]
            him_ref[:, sl * LANES:(sl + 1) * LANES] = hst[1, sl]


def _s5_prompt(u, s5p, n_seq, s_len):
    a_re, a_im, bb_re, bb_im, c_re, c_im, d_skip = s5p
    g, p, gc = bb_re.shape
    nrow = 2 * n_seq
    assert nrow == SUBLANES, "the scan packs (half, sequence) on the eight sublanes"
    bb, cc, a, d, gb = _s5_layout(a_re, a_im, bb_re, bb_im, c_re, c_im, d_skip, nrow)
    nblk = g // gb
    nb_half = nblk // 2
    sb = gb * p
    nslab = nb_half * sb // LANES
    sh = g * p // 2
    length = min(128, s_len)
    pitch = length + SUBLANES
    chains = 4 if nslab % 4 == 0 else 1
    d_ssm = g * gc
    z, hre, him = pl.pallas_call(
        functools.partial(_s5_kernel, n_seq=n_seq, length=length, pitch=pitch, nb_half=nb_half, chains=chains),
        grid=(s_len // length,),
        in_specs=[pl.BlockSpec((n_seq, length, d_ssm), lambda c: (0, c, 0)),
                  pl.BlockSpec(bb.shape, lambda c: (0, 0, 0, 0)),
                  pl.BlockSpec(cc.shape, lambda c: (0, 0, 0)),
                  pl.BlockSpec(a.shape, lambda c: (0, 0, 0, 0)),
                  pl.BlockSpec(d.shape, lambda c: (0, 0))],
        out_specs=[pl.BlockSpec((n_seq, length, d_ssm), lambda c: (0, c, 0)),
                   pl.BlockSpec((nrow, sh), lambda c: (0, 0)),
                   pl.BlockSpec((nrow, sh), lambda c: (0, 0))],
        out_shape=[jax.ShapeDtypeStruct((n_seq, s_len, d_ssm), BF16),
                   jax.ShapeDtypeStruct((nrow, sh), F32),
                   jax.ShapeDtypeStruct((nrow, sh), F32)],
        scratch_shapes=[pltpu.VMEM((nslab, nrow * pitch, LANES), F32),
                        pltpu.VMEM((nslab, nrow * pitch, LANES), F32),
                        pltpu.VMEM((2, nslab, nrow, LANES), F32)],
        compiler_params=_cparams(1),
        name="s5_prompt",
    )(u, bb, cc, a, d)

    def unpack(h):
        return h.reshape(2, n_seq, g // 2, p).transpose(1, 0, 2, 3).reshape(n_seq, g, p)

    return z, unpack(hre), unpack(him)


def _s5_step_kernel(u_ref, h0r_ref, h0i_ref, bb_ref, cc_ref, ar_ref, ai_ref, d_ref, z_ref, hr_ref, hi_ref):
    _, nblk, cb, sb = bb_ref.shape
    u = u_ref[...]
    ub = u.astype(BF16)
    for b in range(nblk):
        lhs = ub[:, b * cb:(b + 1) * cb]
        st = slice(b * sb, (b + 1) * sb)
        x_re = jnp.dot(lhs, bb_ref[0, b], preferred_element_type=F32)
        x_im = jnp.dot(lhs, bb_ref[1, b], preferred_element_type=F32)
        a_re, a_im = ar_ref[:, st], ai_ref[:, st]
        h0r, h0i = h0r_ref[:, st], h0i_ref[:, st]
        h_re = a_re * h0r - a_im * h0i + x_re
        h_im = a_re * h0i + a_im * h0r + x_im
        hr_ref[:, st] = h_re
        hi_ref[:, st] = h_im
        lhs2 = jnp.concatenate([h_re, h_im], axis=1).astype(BF16)
        y = jnp.dot(lhs2, cc_ref[b], preferred_element_type=F32)
        y = y + d_ref[:, b * cb:(b + 1) * cb] * u[:, b * cb:(b + 1) * cb]
        z_ref[:, b * cb:(b + 1) * cb] = _gelu_tanh(y).astype(z_ref.dtype)


def _s5_sample(u, h0_re, h0_im, s5p):
    a_re, a_im, bb_re, bb_im, c_re, c_im, d_skip = s5p
    g, p, gc = bb_re.shape
    db = u.shape[0]
    bb, cc, _, d, _ = _s5_layout(a_re, a_im, bb_re, bb_im, c_re, c_im, d_skip, SUBLANES)
    ns = g * p
    z, hr, hi = pl.pallas_call(
        _s5_step_kernel,
        out_shape=[jax.ShapeDtypeStruct((db, g * gc), BF16),
                   jax.ShapeDtypeStruct((db, ns), F32),
                   jax.ShapeDtypeStruct((db, ns), F32)],
        compiler_params=pltpu.CompilerParams(vmem_limit_bytes=VMEM_LIMIT_BYTES),
        name="s5_sample",
    )(u, h0_re.reshape(db, ns), h0_im.reshape(db, ns), bb, cc,
      a_re.reshape(1, ns), a_im.reshape(1, ns), d)
    return z, hr.reshape(db, g, p), hi.reshape(db, g, p)


def _mix_kernel(z_ref, at_ref, ga_ref, gb_ref, x_ref, wv_ref, wg_ref, wup_ref, wo_ref, g1_ref, b1_ref,
                wr_ref, br_ref, h_ref, route_ref, *, alpha, n_exp):
    z = z_ref[...]
    br_ssm = (jnp.dot(z, wv_ref[...], preferred_element_type=F32)
              * _sigmoid(jnp.dot(z, wg_ref[...], preferred_element_type=F32)))
    br_attn = jnp.dot(at_ref[...].astype(BF16), wup_ref[...], preferred_element_type=F32)
    mix = _sigmoid(ga_ref[...]) * br_ssm + _sigmoid(gb_ref[...]) * br_attn
    r = alpha * x_ref[...] + jnp.dot(mix.astype(BF16), wo_ref[...], preferred_element_type=F32)
    h = _layernorm(r, g1_ref[...], b1_ref[...])
    h_ref[...] = h
    logits = jnp.dot(h.astype(BF16), wr_ref[...], preferred_element_type=F32) + br_ref[...]
    lane = lax.broadcasted_iota(I32, logits.shape, 1)
    lg = jnp.where(lane < n_exp, logits, -jnp.inf)
    vals, ids = [], []
    for _ in range(TOP_K):
        m = jnp.max(lg, axis=1, keepdims=True)
        idx = jnp.min(jnp.where(lg == m, lane, LANES), axis=1, keepdims=True)
        vals.append(m)
        ids.append(idx)
        lg = jnp.where(lane == idx, -jnp.inf, lg)
    ex = [jnp.exp(v - vals[0]) for v in vals]
    den = ex[0]
    for e in ex[1:]:
        den = den + e
    route = jnp.zeros(logits.shape, F32)
    for kk in range(TOP_K):
        route = jnp.where(lane == kk, ids[kk].astype(F32), route)
        route = jnp.where(lane == TOP_K + kk, ex[kk] / den, route)
    route_ref[...] = route


def _mix(z, attn, ga, gb, x, wts, alpha, n_exp, tm):
    t, d = x.shape
    tm = min(tm, t)
    wv, wg, wup, wo, g1, b1, wr, br = wts

    def rows(w):
        return pl.BlockSpec((tm, w), lambda i: (i, 0))

    def whole(a):
        return pl.BlockSpec(a.shape, lambda i: (0,) * a.ndim, pipeline_mode=pl.Buffered(1))

    return pl.pallas_call(
        functools.partial(_mix_kernel, alpha=alpha, n_exp=n_exp),
        grid=(t // tm,),
        in_specs=[rows(z.shape[1]), rows(attn.shape[1]), rows(d), rows(d), rows(d)] + [whole(w) for w in wts],
        out_specs=[rows(d), rows(LANES)],
        out_shape=[jax.ShapeDtypeStruct((t, d), F32), jax.ShapeDtypeStruct((t, LANES), F32)],
        compiler_params=_cparams(1),
        name="mix_ln1_router",
    )(z, attn, ga, gb, x, *wts)


def _rank_kernel(r_ref, rank_ref, cnt_ref, carry):
    @pl.when(pl.program_id(0) == 0)
    def _():
        carry[...] = jnp.zeros_like(carry)

    r = r_ref[...]
    tm = r.shape[0]
    lane = lax.broadcasted_iota(I32, r.shape, 1)
    ids = [r[:, kk:kk + 1].astype(I32) for kk in range(TOP_K)]
    onehot = jnp.zeros(r.shape, F32)
    for e in ids:
        onehot = onehot + (lane == e).astype(F32)
    tri = (lax.broadcasted_iota(I32, (tm, tm), 0) > lax.broadcasted_iota(I32, (tm, tm), 1)).astype(BF16)
    before = jnp.dot(tri, onehot.astype(BF16), preferred_element_type=F32) + carry[...]
    out = jnp.zeros(r.shape, F32)
    for kk, e in enumerate(ids):
        rk = jnp.sum(jnp.where(lane == e, before, 0.0), axis=1, keepdims=True)
        out = jnp.where(lane == kk, rk, out)
    rank_ref[...] = out.astype(I32)
    carry[...] = carry[...] + jnp.sum(onehot, axis=0, keepdims=True)
    cnt_ref[...] = carry[...]


def _ranks(route_all, tm):
    t = route_all.shape[0]
    return pl.pallas_call(
        _rank_kernel,
        grid=(t // tm,),
        in_specs=[pl.BlockSpec((tm, LANES), lambda i: (i, 0))],
        out_specs=[pl.BlockSpec((tm, LANES), lambda i: (i, 0)), pl.BlockSpec((1, LANES), lambda i: (0, 0))],
        out_shape=[jax.ShapeDtypeStruct((t, LANES), I32), jax.ShapeDtypeStruct((1, LANES), F32)],
        scratch_shapes=[pltpu.VMEM((1, LANES), F32)],
        compiler_params=_cparams(1),
        name="moe_rank",
    )(route_all)


def _dispatch_kernel(zflag_ref, dest1_ref, dest2_ref, h1_ref, h2_ref, xs_ref, zero_ref, sem, *, n1):
    i = pl.program_id(0)

    def scatter(h_ref, dest_ref):
        def row_copy(r, d):
            return pltpu.make_async_copy(h_ref.at[pl.ds(r, 1), :], xs_ref.at[pl.ds(d, 1), :], sem)

        def issue(r, _):
            for kk in range(TOP_K):
                row_copy(r, dest_ref[r * TOP_K + kk]).start(priority=kk % DMA_PRIORITIES)
            return 0

        def drain(r, _):
            for _kk in range(TOP_K):
                row_copy(0, 0).wait()
            return 0

        lax.fori_loop(0, h_ref.shape[0], issue, 0)
        lax.fori_loop(0, h_ref.shape[0], drain, 0)

    @pl.when(i == 0)
    def _():
        zero_ref[...] = jnp.zeros_like(zero_ref)

        def zero_block(j):
            r0 = pl.multiple_of(j * MOE_BLOCK, MOE_BLOCK)
            return pltpu.make_async_copy(zero_ref, xs_ref.at[pl.ds(r0, MOE_BLOCK), :], sem)

        def issue(j, _):
            @pl.when(zflag_ref[j] != 0)
            def _():
                zero_block(j).start()
            return 0

        def drain(j, _):
            @pl.when(zflag_ref[j] != 0)
            def _():
                zero_block(0).wait()
            return 0

        lax.fori_loop(0, zflag_ref.shape[0], issue, 0)
        lax.fori_loop(0, zflag_ref.shape[0], drain, 0)

    @pl.when(i < n1)
    def _():
        scatter(h1_ref, dest1_ref)

    @pl.when(i == n1)
    def _():
        scatter(h2_ref, dest2_ref)


def _dispatch(h1, h2, dest1, dest2, zflag, tm):
    t1, d = h1.shape
    n1 = t1 // tm
    per_step = tm * TOP_K
    assert t1 % tm == 0
    smem = pltpu.SMEM
    return pl.pallas_call(
        functools.partial(_dispatch_kernel, n1=n1),
        grid=(n1 + 1,),
        in_specs=[pl.BlockSpec(zflag.shape, lambda i: (0,), memory_space=smem),
                  pl.BlockSpec((per_step,), lambda i: (jnp.minimum(i, n1 - 1),), memory_space=smem),
                  pl.BlockSpec(dest2.shape, lambda i: (0,), memory_space=smem),
                  pl.BlockSpec((tm, d), lambda i: (jnp.minimum(i, n1 - 1), 0)),
                  pl.BlockSpec(h2.shape, lambda i: (0, 0))],
        out_specs=pl.BlockSpec(memory_space=pl.ANY),
        out_shape=jax.ShapeDtypeStruct((zflag.shape[0] * MOE_BLOCK, d), F32),
        scratch_shapes=[pltpu.VMEM((MOE_BLOCK, d), F32), pltpu.SemaphoreType.DMA(())],
        compiler_params=_cparams(1),
        name="moe_dispatch",
    )(zflag, dest1, dest2, h1, h2)


def _expert_weights(be_ref, nu, i, f, nf, fetch, land, wb):
    e = be_ref[i]
    first = (i == 0) | (e != be_ref[jnp.maximum(i - 1, 0)])

    @pl.when(first)
    def _():
        @pl.when((i == 0) & (f == 0))
        def _():
            for cp in fetch(e, f):
                cp.start()

        for cp in fetch(e, f):
            cp.wait()
        for m in range(land.shape[0]):
            wb[m] = land[m].astype(BF16)

        last = be_ref.shape[0] - 1
        nxt = lax.while_loop(lambda j: (j < nu) & (be_ref[jnp.minimum(j, last)] == e), lambda j: j + 1, i + 1)

        @pl.when(nxt < nu)
        def _():
            for cp in fetch(be_ref[jnp.minimum(nxt, last)], f):
                cp.start()

        @pl.when((nxt >= nu) & (f + 1 < nf))
        def _():
            for cp in fetch(be_ref[0], f + 1):
                cp.start()


def _g1_kernel(be_ref, nu_ref, xs_ref, bg_ref, bu_ref, wg_hbm, wu_hbm, act_ref, land, wb, sem):
    f = pl.program_id(0)
    i = pl.program_id(1)
    nu = nu_ref[0]
    tf = land.shape[2]

    def fetch(e, ff):
        c0 = pl.multiple_of(ff * tf, tf)
        return (pltpu.make_async_copy(wg_hbm.at[e, :, pl.ds(c0, tf)], land.at[0], sem.at[0]),
                pltpu.make_async_copy(wu_hbm.at[e, :, pl.ds(c0, tf)], land.at[1], sem.at[1]))

    @pl.when(i < nu)
    def _():
        _expert_weights(be_ref, nu, i, f, pl.num_programs(0), fetch, land, wb)
        x = xs_ref[...].astype(BF16)
        g = jnp.dot(x, wb[0], preferred_element_type=F32) + bg_ref[0]
        up = jnp.dot(x, wb[1], preferred_element_type=F32) + bu_ref[0]
        g = jnp.minimum(g, SWIGLU_LIMIT)
        up = jnp.clip(up, -SWIGLU_LIMIT, SWIGLU_LIMIT)
        act_ref[...] = ((up + 1.0) * (g * _sigmoid(SWIGLU_ALPHA * g))).astype(act_ref.dtype)

    @pl.when(i >= nu)
    def _():
        act_ref[...] = jnp.zeros_like(act_ref)


def _g2_kernel(be_ref, nu_ref, act_ref, bd_ref, wd_hbm, y_ref, land, wb, sem):
    i = pl.program_id(0)
    nu = nu_ref[0]

    def fetch(e, ff):
        del ff
        return (pltpu.make_async_copy(wd_hbm.at[e], land.at[0], sem.at[0]),)

    @pl.when(i < nu)
    def _():
        _expert_weights(be_ref, nu, i, 0, 1, fetch, land, wb)
        y_ref[...] = jnp.dot(act_ref[...], wb[0], preferred_element_type=F32) + bd_ref[0]

    @pl.when(i >= nu)
    def _():
        y_ref[...] = jnp.zeros_like(y_ref)


def _experts(xs, blk_e, n_used, w_gate, b_gate, w_up, b_up, w_down, b_down):
    n_rows, d = xs.shape
    n_exp, _, dff = w_gate.shape
    nblk = n_rows // MOE_BLOCK
    tf = min(1024, dff)
    nf = dff // tf

    def blk_row(f, i, be, nu):
        return (jnp.minimum(i, nu[0] - 1), 0)

    def b_spec(f, i, be, nu):
        return (be[jnp.minimum(i, nu[0] - 1)], 0, f)

    hbm = pl.BlockSpec(memory_space=pl.ANY)
    act = pl.pallas_call(
        _g1_kernel,
        grid_spec=pltpu.PrefetchScalarGridSpec(
            num_scalar_prefetch=2, grid=(nf, nblk),
            in_specs=[pl.BlockSpec((MOE_BLOCK, d), blk_row),
                      pl.BlockSpec((1, 1, tf), b_spec), pl.BlockSpec((1, 1, tf), b_spec), hbm, hbm],
            out_specs=pl.BlockSpec((MOE_BLOCK, tf), lambda f, i, be, nu: (i, f)),
            scratch_shapes=[pltpu.VMEM((2, d, tf), F32), pltpu.VMEM((2, d, tf), BF16),
                            pltpu.SemaphoreType.DMA((2,))]),
        out_shape=jax.ShapeDtypeStruct((n_rows, dff), BF16),
        compiler_params=_cparams(2),
        name="moe_gate_up",
    )(blk_e, n_used, xs, b_gate.reshape(n_exp, 1, dff), b_up.reshape(n_exp, 1, dff), w_gate, w_up)

    return pl.pallas_call(
        _g2_kernel,
        grid_spec=pltpu.PrefetchScalarGridSpec(
            num_scalar_prefetch=2, grid=(nblk,),
            in_specs=[pl.BlockSpec((MOE_BLOCK, dff), lambda i, be, nu: (jnp.minimum(i, nu[0] - 1), 0)),
                      pl.BlockSpec((1, 1, d), lambda i, be, nu: (be[jnp.minimum(i, nu[0] - 1)], 0, 0)), hbm],
            out_specs=pl.BlockSpec((MOE_BLOCK, d), lambda i, be, nu: (i, 0)),
            scratch_shapes=[pltpu.VMEM((1, dff, d), F32), pltpu.VMEM((1, dff, d), BF16),
                            pltpu.SemaphoreType.DMA((1,))]),
        out_shape=jax.ShapeDtypeStruct((n_rows, d), F32),
        compiler_params=_cparams(1),
        name="moe_down",
    )(blk_e, n_used, act, b_down.reshape(n_exp, 1, d), w_down)


def _combine_kernel(dest_ref, h_ref, route_ref, g_ref, b_ref, yb_ref, o_ref, buf, sem, *, alpha):
    tm = h_ref.shape[0]

    def row_copy(r, kk, d):
        return pltpu.make_async_copy(yb_ref.at[pl.ds(d, 1), :], buf.at[kk, pl.ds(r, 1), :], sem)

    def issue(r, _):
        for kk in range(TOP_K):
            row_copy(r, kk, dest_ref[r * TOP_K + kk]).start(priority=kk % DMA_PRIORITIES)
        return 0

    def drain(r, _):
        for kk in range(TOP_K):
            row_copy(0, kk, 0).wait()
        return 0

    lax.fori_loop(0, tm, issue, 0)
    lax.fori_loop(0, tm, drain, 0)
    route = route_ref[...]
    acc = alpha * h_ref[...]
    for kk in range(TOP_K):
        acc = acc + route[:, TOP_K + kk:TOP_K + kk + 1] * buf[kk]
    o_ref[...] = _layernorm(acc, g_ref[...], b_ref[...])


def _combine(h, route, dest, yb, g2, b2, alpha, tm):
    t, d = h.shape
    tm = min(tm, t)
    return pl.pallas_call(
        functools.partial(_combine_kernel, alpha=alpha),
        grid=(t // tm,),
        in_specs=[pl.BlockSpec((tm * TOP_K,), lambda i: (i,), memory_space=pltpu.SMEM),
                  pl.BlockSpec((tm, d), lambda i: (i, 0)),
                  pl.BlockSpec((tm, LANES), lambda i: (i, 0)),
                  pl.BlockSpec((1, d), lambda i: (0, 0)),
                  pl.BlockSpec((1, d), lambda i: (0, 0)),
                  pl.BlockSpec(memory_space=pl.ANY)],
        out_specs=pl.BlockSpec((tm, d), lambda i: (i, 0)),
        out_shape=jax.ShapeDtypeStruct((t, d), F32),
        scratch_shapes=[pltpu.VMEM((TOP_K, tm, d), F32), pltpu.SemaphoreType.DMA(())],
        compiler_params=_cparams(1),
        name="moe_combine_ln2",
    )(dest, h, route, g2, b2, yb)


def _layer(x_prompt, x_sample, cache_k, cache_v, page_table, st_re, st_im, w_in,
           lam_re, lam_im, log_dt, b_re, b_im, c_re, c_im, d_skip,
           w_glu_val, w_glu_gate, w_attn_up, w_o, ln1_g, ln1_b,
           w_router, b_router, w_gate, b_gate, w_up, b_up, w_down, b_down, ln2_g, ln2_b, depth):
    n_seq, s_len, d_model = x_prompt.shape
    db, dec_seq, _ = x_sample.shape
    assert dec_seq == 1, "decode path handles one new token per sequence"
    n_phys, page, n_heads, hd = cache_k.shape
    d_attn = n_heads * hd
    n_groups, n_state = st_re.shape[1:]
    d_ssm = n_groups * b_re.shape[-1]
    n_exp = w_router.shape[1]
    alpha = (2 * depth) ** 0.25
    tp, ts = n_seq * s_len, db
    past = page_table.shape[1] * page
    col_q, col_k, col_v, col_u = 0, d_attn, 2 * d_attn, 3 * d_attn
    col_ga, col_gb = 3 * d_attn + d_ssm, 3 * d_attn + d_ssm + d_model

    wb = w_in.astype(BF16)
    s5p = _s5_discretise(lam_re.astype(F32), lam_im.astype(F32), log_dt, b_re.astype(F32), b_im.astype(F32))
    s5p = s5p + (c_re.astype(F32), c_im.astype(F32), d_skip)
    wr_pad = jnp.zeros((d_model, LANES), BF16).at[:, :n_exp].set(w_router.astype(BF16))
    br_pad = jnp.zeros((1, LANES), F32).at[0, :n_exp].set(b_router.astype(F32))
    mix_w = (w_glu_val.astype(BF16), w_glu_gate.astype(BF16), w_attn_up.astype(BF16), w_o.astype(BF16),
             ln1_g.reshape(1, d_model).astype(F32), ln1_b.reshape(1, d_model).astype(F32), wr_pad, br_pad)

    def project(x2, pos, reps, tm, tag):
        xb = x2.astype(BF16)
        tabs = tuple(jnp.tile(tab, (reps, 1)) for tab in _rope_tables(pos, hd))
        kw = dict(tm=tm, tn=1024)
        q = _proj(xb, wb, col_q, d_attn, BF16 if tag == "p" else F32, rope_tabs=tabs, hd=hd, name=f"proj_q_{tag}", **kw)
        k = _proj(xb, wb, col_k, d_attn, F32, rope_tabs=tabs, hd=hd, name=f"proj_k_{tag}", **kw)
        v = _proj(xb, wb, col_v, d_attn, F32, name=f"proj_v_{tag}", **kw)
        u = _proj(xb, wb, col_u, d_ssm, F32, name=f"proj_u_{tag}", **kw)
        ga = _proj(xb, wb, col_ga, d_model, F32, name=f"proj_ga_{tag}", **kw)
        gb = _proj(xb, wb, col_gb, d_model, F32, name=f"proj_gb_{tag}", **kw)
        return q, k, v, u, ga, gb

    xp2 = x_prompt.reshape(tp, d_model)
    q1, k1, v1, u1, ga1, gb1 = project(xp2, jnp.arange(s_len), n_seq, 1024, "p")
    attn1, ksum = _moba_prefill(q1, k1, v1, n_seq, s_len, n_heads, hd, page_table, cache_k)
    z1, hr1, hi1 = _s5_prompt(u1.reshape(n_seq, s_len, d_ssm), s5p, n_seq, s_len)
    h1, route1 = _mix(z1.reshape(tp, d_ssm), attn1, ga1, gb1, xp2, mix_w, alpha, n_exp, 256)

    xs2 = x_sample.reshape(ts, d_model)
    q2, k2, v2, u2, ga2, gb2 = project(xs2, jnp.full((1,), past, I32), ts, ts, "s")
    attn2 = _moba_decode(q2.reshape(ts, n_heads, hd), k2.reshape(ts, n_heads, hd), v2.reshape(ts, n_heads, hd),
                         cache_k, cache_v, page_table, ksum)
    z2, hr2, hi2 = _s5_sample(u2, st_re, st_im, s5p)
    h2, route2 = _mix(z2, attn2.reshape(ts, d_attn), ga2, gb2, xs2, mix_w, alpha, n_exp, ts)

    tr = 256
    t_all = tp + ts
    t_pad = -(-t_all // tr) * tr
    route_all = jnp.concatenate([route1, route2, jnp.full((t_pad - t_all, LANES), -1.0, F32)], axis=0)
    rank_all, cnt = _ranks(route_all, tr)
    counts = cnt[0, :n_exp].astype(I32)
    padded = (counts + MOE_BLOCK - 1) // MOE_BLOCK * MOE_BLOCK
    pad_end = jnp.cumsum(padded)
    pad_start = pad_end - padded
    ids = route_all[:t_all, :TOP_K].astype(I32)
    dest = (pad_start[ids] + rank_all[:t_all, :TOP_K]).astype(I32)
    nblk = -(-(t_all * TOP_K) // MOE_BLOCK) + n_exp
    blk_first_row = jnp.arange(nblk, dtype=I32) * MOE_BLOCK
    blk_e = jnp.minimum(jnp.sum(pad_end[None, :] <= blk_first_row[:, None], axis=1), n_exp - 1).astype(I32)
    n_used = (pad_end[-1:] // MOE_BLOCK).astype(I32)
    dest1 = dest[:tp].reshape(-1)
    dest2 = dest[tp:].reshape(-1)

    blk_idx = jnp.arange(nblk, dtype=I32)
    next_e = jnp.concatenate([blk_e[1:], jnp.full((1,), -1, I32)])
    zflag = ((blk_idx >= n_used[0] - 1) | (blk_e != next_e)).astype(I32)
    xs = _dispatch(h1, h2, dest1, dest2, zflag, 256)
    yb = _experts(xs, blk_e, n_used, w_gate, b_gate, w_up, b_up, w_down, b_down)
    g2 = ln2_g.reshape(1, d_model).astype(F32)
    b2 = ln2_b.reshape(1, d_model).astype(F32)
    y1 = _combine(h1, route1, dest1, yb, g2, b2, alpha, 256)
    y2 = _combine(h2, route2, dest2, yb, g2, b2, alpha, ts)

    return (y1.reshape(n_seq, s_len, d_model), y2.reshape(db, 1, d_model),
            k1.reshape(n_seq, s_len, n_heads, hd), v1.reshape(n_seq, s_len, n_heads, hd), hr1, hi1,
            k2.reshape(db, 1, n_heads, hd), v2.reshape(db, 1, n_heads, hd), hr2, hi2)


def kernel(x_prompt, x_sample, cache_k, cache_v, page_table, state_ssm_re, state_ssm_im, w_in, ssm_lambda_re, ssm_lambda_im, ssm_log_dt, ssm_b_re, ssm_b_im, ssm_c_re, ssm_c_im, ssm_d, w_glu_val, w_glu_gate, w_attn_up, w_o, ln1_g, ln1_b, w_router, b_router, w_gate, b_gate, w_up, b_up, w_down, b_down, ln2_g, ln2_b):
    depth = w_in.shape[0]
    assert depth == 1, "single-layer step"
    outs = _layer(x_prompt, x_sample, cache_k[0], cache_v[0], page_table, state_ssm_re[0], state_ssm_im[0],
                  w_in[0], ssm_lambda_re[0], ssm_lambda_im[0], ssm_log_dt[0], ssm_b_re[0], ssm_b_im[0],
                  ssm_c_re[0], ssm_c_im[0], ssm_d[0], w_glu_val[0], w_glu_gate[0], w_attn_up[0], w_o[0],
                  ln1_g[0], ln1_b[0], w_router[0], b_router[0], w_gate[0], b_gate[0], w_up[0], b_up[0],
                  w_down[0], b_down[0], ln2_g[0], ln2_b[0], depth)
    y1, y2, k1, v1, hr1, hi1, k2, v2, hr2, hi2 = outs
    return (y1, y2, k1[None], v1[None], hr1[None], hi1[None], k2[None], v2[None], hr2[None], hi2[None])
```

```python
import functools
import math

import jax
import jax.numpy as jnp
from jax import lax
from jax.experimental import pallas as pl
from jax.experimental.pallas import tpu as pltpu

F32, BF16, I32 = jnp.float32, jnp.bfloat16, jnp.int32

ROPE_THETA = 500000.0
ROT_FRACTION = 4
MOBA_BLOCK = 256
MOBA_TOPK = 3
TOP_K = 4
SWIGLU_LIMIT = 7.0
SWIGLU_ALPHA = 1.702
LN_EPS = 1e-5
MOE_BLOCK = 512

LANES = 128
SUBLANES = 8
VMEM_LIMIT_BYTES = 56 * 1024 * 1024
DMA_PRIORITIES = 2

NEG_BIG = -1e30


def _cparams(n_axes):
    return pltpu.CompilerParams(dimension_semantics=("arbitrary",) * n_axes,
                                vmem_limit_bytes=VMEM_LIMIT_BYTES)


def _sigmoid(x):
    return 1.0 / (1.0 + jnp.exp(-x))


def _gelu_tanh(x):
    c = math.sqrt(2.0 / math.pi)
    return x * (0.5 * (1.0 + jnp.tanh(c * (x + 0.044715 * (x * x * x)))))


def _layernorm(r, g, b):
    mu = jnp.mean(r, axis=-1, keepdims=True)
    xc = r - mu
    var = jnp.mean(xc * xc, axis=-1, keepdims=True)
    return xc * lax.rsqrt(var + LN_EPS) * g + b


def _proj_kernel(x_ref, w_ref, *rest, rope, rot_half, hd):
    acc = jnp.dot(x_ref[...], w_ref[...], preferred_element_type=F32)
    if not rope:
        (o_ref,) = rest
        o_ref[...] = acc.astype(o_ref.dtype)
        return
    cos_ref, sin_ref, o_ref = rest
    cos = cos_ref[...]
    sin = sin_ref[...]
    first = lax.broadcasted_iota(I32, cos.shape, 1) < rot_half
    for hh in range(acc.shape[1] // hd):
        xh = acc[:, hh * hd:(hh + 1) * hd]
        rot = jnp.where(first, pltpu.roll(xh, hd - rot_half, 1), pltpu.roll(xh, rot_half, 1))
        o_ref[:, hh * hd:(hh + 1) * hd] = (xh * cos + rot * sin).astype(o_ref.dtype)


def _proj(xb, wb, col0, ncols, out_dtype, *, tm, tn, rope_tabs=None, hd=LANES, name):
    t, k = xb.shape
    tm = min(tm, t)
    tn = min(tn, ncols)
    off = col0 // tn
    assert col0 % tn == 0 and ncols % tn == 0 and t % tm == 0
    in_specs = [pl.BlockSpec((tm, k), lambda i, j: (i, 0)),
                pl.BlockSpec((k, tn), lambda i, j: (0, j + off))]
    args = [xb, wb]
    if rope_tabs is not None:
        in_specs += [pl.BlockSpec((tm, hd), lambda i, j: (i, 0))] * 2
        args += list(rope_tabs)
    return pl.pallas_call(
        functools.partial(_proj_kernel, rope=rope_tabs is not None, rot_half=hd // ROT_FRACTION // 2, hd=hd),
        grid=(t // tm, ncols // tn),
        in_specs=in_specs,
        out_specs=pl.BlockSpec((tm, tn), lambda i, j: (i, j)),
        out_shape=jax.ShapeDtypeStruct((t, ncols), out_dtype),
        compiler_params=_cparams(2),
        name=name,
    )(*args)


def _rope_tables(pos, hd):
    rot = hd // ROT_FRACTION
    half = rot // 2
    inv = ROPE_THETA ** (-jnp.arange(half, dtype=F32) * 2.0 / rot)
    ang = pos.astype(F32)[:, None] * inv[None, :]
    cos, sin = jnp.cos(ang), jnp.sin(ang)
    t = pos.shape[0]
    cosf = jnp.concatenate([cos, cos, jnp.ones((t, hd - rot), F32)], axis=1)
    sinf = jnp.concatenate([-sin, sin, jnp.zeros((t, hd - rot), F32)], axis=1)
    return cosf, sinf


def _page_sums(page_refs, o_ref, ppb):
    page = page_refs[0].shape[1]
    chains = 16 if page % 16 == 0 else 1
    for r in range(len(page_refs) // ppb):
        x = None
        for u in range(ppb):
            ref = page_refs[r * ppb + u]
            for c in range(page // chains):
                piece = ref[0, c * chains:(c + 1) * chains]
                x = piece if x is None else x + piece
        o_ref[0, r] = jnp.sum(x, axis=0)


def _attn_kernel(pt_ref, q_ref, k_ref, v_ref, *rest, blk, nb, grp, scale, npg, ppb, ksteps):
    del pt_ref
    page_refs = rest[:npg]
    if npg:
        o_ref, ks_ref, ka_ref, vb_ref, km_ref, s_ref = rest[npg:]
        step = (pl.program_id(0) * pl.num_programs(1) + pl.program_id(1)) * pl.num_programs(2) + pl.program_id(2)

        @pl.when(step < ksteps)
        def _():
            _page_sums(page_refs, ks_ref, ppb)
    else:
        o_ref, ka_ref, vb_ref, km_ref, s_ref = rest

    qi = pl.program_id(2)
    s_len, hd = k_ref.shape

    @pl.when(qi == 0)
    def _():
        vb_ref[...] = v_ref[...].astype(BF16)
        ka_ref[:, :hd] = k_ref[...].astype(BF16)
        row = lax.broadcasted_iota(I32, (s_len, hd), 0)
        lane = lax.broadcasted_iota(I32, (s_len, hd), 1)
        member = (row >= lane * blk) & (row < (lane + 1) * blk)
        ka_ref[:, hd:] = jnp.where(member, 1.0, 0.0).astype(BF16)
        km_ref[...] = jnp.zeros_like(km_ref)
        for j in range(nb):
            km_ref[j:j + 1, :] = jnp.mean(k_ref[j * blk:(j + 1) * blk, :], axis=0, keepdims=True)

    q = q_ref[...]
    tq = q.shape[0]
    nt = (((1,), (1,)), ((), ()))
    nbp = -(-nb // SUBLANES) * SUBLANES
    gate_t = lax.dot_general(km_ref[:nbp, :], q.astype(F32), nt,
                             precision=lax.Precision.HIGHEST, preferred_element_type=F32)
    blk_id = lax.broadcasted_iota(I32, (nbp, tq), 0)
    g = jnp.where(blk_id < qi, gate_t, -jnp.inf)
    sel = jnp.zeros(g.shape, jnp.bool_)
    for _ in range(MOBA_TOPK):
        m = jnp.max(g, axis=0, keepdims=True)
        idx = jnp.min(jnp.where(g == m, blk_id, nbp), axis=0, keepdims=True)
        hit = blk_id == idx
        sel = sel | (hit & (m > -jnp.inf))
        g = jnp.where(hit, -jnp.inf, g)
    bias_t = jnp.concatenate([jnp.where(sel, 0.0, NEG_BIG), jnp.full((hd - nbp, tq), NEG_BIG, F32)], axis=0)
    qa = jnp.concatenate([q, bias_t.T.astype(BF16)], axis=1)
    gw = grp * blk
    ntile = blk // LANES

    def lane_fold(x, acc, op):
        for c in range(x.shape[1] // LANES):
            acc = op(acc, x[:, c * LANES:(c + 1) * LANES])
        return acc

    d0 = pl.multiple_of(qi * blk, blk)
    s_own = lax.dot_general(q, ka_ref[pl.ds(d0, blk), :hd], nt, preferred_element_type=F32) * scale
    row = lax.broadcasted_iota(I32, (tq, blk), 0)
    col = lax.broadcasted_iota(I32, (tq, blk), 1)
    s_own = jnp.where(col <= row, s_own, -jnp.inf)
    m_own = lane_fold(s_own[:, LANES:], s_own[:, :LANES], jnp.maximum) if ntile > 1 else s_own

    def attend(ng):
        m_part = m_own
        for gi in range(ng):
            s = lax.dot_general(qa, ka_ref[gi * gw:(gi + 1) * gw, :], nt, preferred_element_type=F32) * scale
            s_ref[gi] = s
            m_part = lane_fold(s, m_part, jnp.maximum)
        m = jnp.max(m_part, axis=1, keepdims=True)
        p_own = jnp.exp(s_own - m)
        l_part = lane_fold(p_own[:, LANES:], p_own[:, :LANES], jnp.add) if ntile > 1 else p_own
        acc = jnp.dot(p_own.astype(BF16), vb_ref[pl.ds(d0, blk), :], preferred_element_type=F32)
        for gi in range(ng):
            p = jnp.exp(s_ref[gi] - m)
            acc = acc + jnp.dot(p.astype(BF16), vb_ref[gi * gw:(gi + 1) * gw, :], preferred_element_type=F32)
            l_part = lane_fold(p, l_part, jnp.add)
        o_ref[...] = (acc / jnp.sum(l_part, axis=1, keepdims=True)).astype(o_ref.dtype)

    ng = lax.div(qi + (grp - 1), grp)
    for n_groups in range((nb - 1 + grp - 1) // grp + 1):
        pl.when(ng == n_groups)(functools.partial(attend, n_groups))


def _ksum_plan(page_table, cache_k):
    db, n_pages = page_table.shape
    page = cache_k.shape[1]
    ppb = MOBA_BLOCK // page
    assert MOBA_BLOCK % page == 0 and n_pages % ppb == 0 and n_pages >= ppb
    npg = 8 if n_pages % 8 == 0 else ppb
    return db, n_pages // ppb, ppb, npg, n_pages // npg


def _moba_prefill(q, k, v, n_seq, s_len, n_heads, hd, page_table, cache_k):
    blk = MOBA_BLOCK
    nb = s_len // blk
    assert s_len % blk == 0 and nb <= hd
    t = n_seq * s_len
    grp = 4 if nb % 4 == 0 else 1
    db, nbp, ppb, npg, nstep = _ksum_plan(page_table, cache_k)
    ksteps = db * nstep
    fuse = ksteps <= n_seq * n_heads * nb
    if not fuse:
        npg = 0

    def ks_step(n, h, i):
        return jnp.minimum((n * n_heads + h) * nb + i, ksteps - 1)

    def page_spec(r):
        return pl.BlockSpec((1,) + cache_k.shape[1:], lambda n, h, i, pg: (pg[ks_step(n, h, i), r], 0, 0, 0))

    out_specs = [pl.BlockSpec((blk, hd), lambda n, h, i, pg: (n * nb + i, h))]
    out_shape = [jax.ShapeDtypeStruct((t, n_heads * hd), BF16)]
    if fuse:
        out_specs.append(pl.BlockSpec((1, npg // ppb) + cache_k.shape[2:],
                                      lambda n, h, i, pg: (ks_step(n, h, i), 0, 0, 0)))
        out_shape.append(jax.ShapeDtypeStruct((ksteps, npg // ppb) + cache_k.shape[2:], F32))
    outs = pl.pallas_call(
        functools.partial(_attn_kernel, blk=blk, nb=nb, grp=grp, scale=1.0 / math.sqrt(hd),
                          npg=npg, ppb=ppb, ksteps=ksteps),
        grid_spec=pltpu.PrefetchScalarGridSpec(
            num_scalar_prefetch=1, grid=(n_seq, n_heads, nb),
            in_specs=[pl.BlockSpec((blk, hd), lambda n, h, i, pt: (n * nb + i, h)),
                      pl.BlockSpec((s_len, hd), lambda n, h, i, pt: (n, h)),
                      pl.BlockSpec((s_len, hd), lambda n, h, i, pt: (n, h))] + [page_spec(r) for r in range(npg)],
            out_specs=out_specs,
            scratch_shapes=[pltpu.VMEM((s_len, 2 * hd), BF16), pltpu.VMEM((s_len, hd), BF16),
                            pltpu.VMEM((hd, hd), F32), pltpu.VMEM((nb // grp, blk, grp * blk), F32)]),
        out_shape=out_shape,
        compiler_params=_cparams(3),
        name="moba_prefill",
    )(page_table.reshape(ksteps, -1), q, k, v, *([cache_k] * npg))
    return (outs[0], outs[1].reshape((db, nbp) + cache_k.shape[2:])) if fuse else (outs[0], None)


def _dec_ksum_kernel(pt_ref, *refs, npg, ppb):
    del pt_ref
    _page_sums(refs[:npg], refs[npg], ppb)


def _dec_sel_kernel(ks_ref, q_ref, o_ref, *, blk):
    km = ks_ref[0] * (1.0 / blk)
    nbp, n_heads, _ = km.shape
    g = jnp.sum(km * q_ref[...], axis=2, keepdims=True)
    blk_id = lax.broadcasted_iota(I32, g.shape, 0)
    lane = lax.broadcasted_iota(I32, (n_heads, LANES), 1)
    out = jnp.full((n_heads, LANES), nbp, I32)
    for r in range(MOBA_TOPK):
        m = jnp.max(g, axis=0, keepdims=True)
        idx = jnp.min(jnp.where(g == m, blk_id, nbp), axis=0, keepdims=True)
        pick = jnp.where(m > -jnp.inf, idx, nbp)[0]
        out = jnp.where(lane == r, pick, out)
        g = jnp.where(blk_id == idx, -jnp.inf, g)
    o_ref[0] = out


def _dec_attn_kernel(pt_ref, sel_ref, q_ref, kn_ref, vn_ref, ck_ref, cv_ref, o_ref, kbuf, vbuf, sem,
                     *, nsel, ppb, nbp, scale):
    b = pl.program_id(0)
    nb = pl.num_programs(0)
    n_heads = q_ref.shape[1]
    npg = nsel * ppb

    def copies(bb, slot):
        out = []
        for h in range(n_heads):
            for r in range(nsel):
                blk_id = jnp.minimum(sel_ref[(bb * n_heads + h) * nsel + r], nbp - 1)
                for u in range(ppb):
                    phys = pt_ref[bb, blk_id * ppb + u]
                    j = h * npg + r * ppb + u
                    out.append(pltpu.make_async_copy(ck_ref.at[phys, :, h, :], kbuf.at[slot, j], sem.at[slot]))
                    out.append(pltpu.make_async_copy(cv_ref.at[phys, :, h, :], vbuf.at[slot, j], sem.at[slot]))
        return out

    slot = lax.rem(b, 2)

    @pl.when(b == 0)
    def _():
        for cp in copies(b, 0):
            cp.start()

    @pl.when(b + 1 < nb)
    def _():
        for cp in copies(b + 1, 1 - slot):
            cp.start()

    for cp in copies(b, slot):
        cp.wait()

    q8 = q_ref[0]
    page = kbuf.shape[2]
    for h in range(n_heads):
        ks = jnp.concatenate([kbuf[slot, h * npg + j] for j in range(npg)], axis=0)
        vs = jnp.concatenate([vbuf[slot, h * npg + j] for j in range(npg)], axis=0)
        s_all = lax.dot_general(q8.astype(BF16), ks.astype(BF16), (((1,), (1,)), ((), ())),
                                preferred_element_type=F32)
        s = s_all[h:h + 1] * scale
        col = lax.broadcasted_iota(I32, s.shape, 1)
        ok = jnp.zeros(s.shape, jnp.bool_)
        for r in range(nsel):
            in_seg = (col >= r * ppb * page) & (col < (r + 1) * ppb * page)
            ok = ok | (in_seg & (sel_ref[(b * n_heads + h) * nsel + r] < nbp))
        s = jnp.where(ok, s, -jnp.inf)
        qh, knh, vnh = q8[h:h + 1], kn_ref[0, h:h + 1], vn_ref[0, h:h + 1]
        s_new = jnp.sum(qh * knh, axis=1, keepdims=True) * scale
        m = jnp.maximum(jnp.max(s, axis=1, keepdims=True), s_new)
        p = jnp.exp(s - m)
        p_new = jnp.exp(s_new - m)
        den = jnp.sum(p, axis=1, keepdims=True) + p_new
        out = jnp.dot(p.astype(BF16), vs.astype(BF16), preferred_element_type=F32) + p_new * vnh
        o_ref[0, h:h + 1, :] = out / den


def _moba_decode(q, k_new, v_new, cache_k, cache_v, page_table, ksum):
    _, page, n_heads, hd = cache_k.shape
    blk = MOBA_BLOCK
    db, nbp, ppb, npg, nstep = _ksum_plan(page_table, cache_k)

    if ksum is None:
        def page_spec(r):
            return pl.BlockSpec((1, page, n_heads, hd), lambda b, s, pt: (pt[b, s * npg + r], 0, 0, 0))

        ksum = pl.pallas_call(
            functools.partial(_dec_ksum_kernel, npg=npg, ppb=ppb),
            grid_spec=pltpu.PrefetchScalarGridSpec(
                num_scalar_prefetch=1, grid=(db, nstep),
                in_specs=[page_spec(r) for r in range(npg)],
                out_specs=pl.BlockSpec((1, npg // ppb, n_heads, hd), lambda b, s, pt: (b, s, 0, 0))),
            out_shape=jax.ShapeDtypeStruct((db, nbp, n_heads, hd), F32),
            compiler_params=_cparams(2),
            name="dec_ksum",
        )(page_table, *([cache_k] * npg))

    sel = pl.pallas_call(
        functools.partial(_dec_sel_kernel, blk=blk),
        grid=(db,),
        in_specs=[pl.BlockSpec((1, nbp, n_heads, hd), lambda b: (b, 0, 0, 0)),
                  pl.BlockSpec((1, n_heads, hd), lambda b: (b, 0, 0))],
        out_specs=pl.BlockSpec((1, n_heads, LANES), lambda b: (b, 0, 0)),
        out_shape=jax.ShapeDtypeStruct((db, n_heads, LANES), I32),
        compiler_params=_cparams(1),
        name="dec_select",
    )(ksum, q)
    nsel = MOBA_TOPK
    sel = sel[:, :, :nsel].reshape(-1)

    row_spec = pl.BlockSpec((1, n_heads, hd), lambda b, pt, sl: (b, 0, 0))
    n_buf = n_heads * nsel * ppb
    return pl.pallas_call(
        functools.partial(_dec_attn_kernel, nsel=nsel, ppb=ppb, nbp=nbp, scale=1.0 / math.sqrt(hd)),
        grid_spec=pltpu.PrefetchScalarGridSpec(
            num_scalar_prefetch=2, grid=(db,),
            in_specs=[row_spec, row_spec, row_spec,
                      pl.BlockSpec(memory_space=pl.ANY), pl.BlockSpec(memory_space=pl.ANY)],
            out_specs=row_spec,
            scratch_shapes=[pltpu.VMEM((2, n_buf, page, hd), F32), pltpu.VMEM((2, n_buf, page, hd), F32),
                            pltpu.SemaphoreType.DMA((2,))]),
        out_shape=jax.ShapeDtypeStruct((db, n_heads, hd), F32),
        compiler_params=_cparams(1),
        name="dec_attend",
    )(page_table, sel, q, k_new, v_new, cache_k, cache_v)


def _s5_discretise(lam_re, lam_im, log_dt, b_re, b_im):
    dt = jnp.exp(log_dt.astype(F32))[:, None]
    mag = jnp.exp(lam_re * dt)
    ang = lam_im * dt
    a_re = mag * jnp.cos(ang)
    a_im = mag * jnp.sin(ang)
    den = lam_re * lam_re + lam_im * lam_im
    nr = a_re - 1.0
    f_re = (nr * lam_re + a_im * lam_im) / den
    f_im = (a_im * lam_re - nr * lam_im) / den
    bb_re = f_re[..., None] * b_re - f_im[..., None] * b_im
    bb_im = f_re[..., None] * b_im + f_im[..., None] * b_re
    return a_re, a_im, bb_re, bb_im


def _s5_layout(a_re, a_im, bb_re, bb_im, c_re, c_im, d_skip, n_rows):
    g, p, gc = bb_re.shape
    gb = max(1, min(g // 2, (2 * LANES) // gc, 16))
    nblk = g // gb
    eye = jnp.eye(gb, dtype=F32)

    def in_blocks(bb):
        return jnp.einsum('bgpc,gh->bgchp', bb.reshape(nblk, gb, p, gc), eye).reshape(nblk, gb * gc, gb * p)

    def out_blocks(c):
        return jnp.einsum('bgcp,gh->bgphc', c.reshape(nblk, gb, gc, p), eye).reshape(nblk, gb * p, gb * gc)

    bb = jnp.stack([in_blocks(bb_re), in_blocks(bb_im)]).astype(BF16)
    cc = jnp.concatenate([out_blocks(c_re), -out_blocks(c_im)], axis=1).astype(BF16)
    sh = g * p // 2
    a = jnp.stack([a_re.reshape(2, sh), a_im.reshape(2, sh)])
    a = jnp.repeat(a, n_rows // 2, axis=1)
    a = a.reshape(2, n_rows, sh // LANES, LANES).transpose(0, 2, 1, 3)
    return bb, cc, a, d_skip.reshape(1, g * gc).astype(F32), gb


def _s5_kernel(u_ref, bb_ref, cc_ref, a_ref, d_ref, z_ref, hre_ref, him_ref, xre, xim, hst,
               *, n_seq, length, pitch, nb_half, chains):
    c = pl.program_id(0)
    nrow = 2 * n_seq
    _, nblk, cb, sb = bb_ref.shape
    nsl = sb // LANES
    nslab = nb_half * nsl

    @pl.when(c == 0)
    def _():
        hst[...] = jnp.zeros_like(hst)

    u = u_ref[...].reshape(n_seq * length, u_ref.shape[2])
    ub = u.astype(BF16)
    for half in range(2):
        for b2 in range(nb_half):
            b = half * nb_half + b2
            lhs = ub[:, b * cb:(b + 1) * cb]
            for ri, xs in ((0, xre), (1, xim)):
                res = jnp.dot(lhs, bb_ref[ri, b], preferred_element_type=F32)
                for n in range(n_seq):
                    for s8 in range(nsl):
                        xs[b2 * nsl + s8, pl.ds((half * n_seq + n) * pitch, length), :] = (
                            res[n * length:(n + 1) * length, s8 * LANES:(s8 + 1) * LANES])

    for cg in range(nslab // chains):
        slabs = [cg * chains + kk for kk in range(chains)]
        ar = [a_ref[0, sl] for sl in slabs]
        ai = [a_ref[1, sl] for sl in slabs]
        init = (tuple(hst[0, sl] for sl in slabs), tuple(hst[1, sl] for sl in slabs))

        def step(t, carry, slabs=slabs, ar=ar, ai=ai):
            hr, hi = carry
            nr, ni = [], []
            for kk, sl in enumerate(slabs):
                rows = pl.ds(t, nrow, stride=pitch)
                xr = xre[sl, rows, :]
                xi = xim[sl, rows, :]
                r = ar[kk] * hr[kk] - ai[kk] * hi[kk] + xr
                i = ar[kk] * hi[kk] + ai[kk] * hr[kk] + xi
                xre[sl, rows, :] = r
                xim[sl, rows, :] = i
                nr.append(r)
                ni.append(i)
            return tuple(nr), tuple(ni)

        hr, hi = lax.fori_loop(0, length, step, init, unroll=2)
        for kk, sl in enumerate(slabs):
            hst[0, sl] = hr[kk]
            hst[1, sl] = hi[kk]

    for half in range(2):
        for b2 in range(nb_half):
            b = half * nb_half + b2

            def gather(xs):
                return jnp.concatenate(
                    [jnp.concatenate([xs[b2 * nsl + s8, pl.ds((half * n_seq + n) * pitch, length), :]
                                      for s8 in range(nsl)], axis=1) for n in range(n_seq)], axis=0)

            lhs = jnp.concatenate([gather(xre), gather(xim)], axis=1).astype(BF16)
            y = jnp.dot(lhs, cc_ref[b], preferred_element_type=F32)
            y = y + d_ref[:, b * cb:(b + 1) * cb] * u[:, b * cb:(b + 1) * cb]
            zz = _gelu_tanh(y).astype(z_ref.dtype)
            for n in range(n_seq):
                z_ref[n, :, b * cb:(b + 1) * cb] = zz[n * length:(n + 1) * length]

    @pl.when(c == pl.num_programs(0) - 1)
    def _():
        for sl in range(nslab):
            hre_ref[:, sl * LANES:(sl + 1) * LANES] = hst[0, sl]
            him_ref[:, sl * LANES:(sl + 1) * LANES] = hst[1, sl]


def _s5_prompt(u, s5p, n_seq, s_len):
    a_re, a_im, bb_re, bb_im, c_re, c_im, d_skip = s5p
    g, p, gc = bb_re.shape
    nrow = 2 * n_seq
    assert nrow == SUBLANES, "the scan packs (half, sequence) on the eight sublanes"
    bb, cc, a, d, gb = _s5_layout(a_re, a_im, bb_re, bb_im, c_re, c_im, d_skip, nrow)
    nblk = g // gb
    nb_half = nblk // 2
    sb = gb * p
    nslab = nb_half * sb // LANES
    sh = g * p // 2
    length = min(128, s_len)
    pitch = length + SUBLANES
    chains = 4 if nslab % 4 == 0 else 1
    d_ssm = g * gc
    z, hre, him = pl.pallas_call(
        functools.partial(_s5_kernel, n_seq=n_seq, length=length, pitch=pitch, nb_half=nb_half, chains=chains),
        grid=(s_len // length,),
        in_specs=[pl.BlockSpec((n_seq, length, d_ssm), lambda c: (0, c, 0)),
                  pl.BlockSpec(bb.shape, lambda c: (0, 0, 0, 0)),
                  pl.BlockSpec(cc.shape, lambda c: (0, 0, 0)),
                  pl.BlockSpec(a.shape, lambda c: (0, 0, 0, 0)),
                  pl.BlockSpec(d.shape, lambda c: (0, 0))],
        out_specs=[pl.BlockSpec((n_seq, length, d_ssm), lambda c: (0, c, 0)),
                   pl.BlockSpec((nrow, sh), lambda c: (0, 0)),
                   pl.BlockSpec((nrow, sh), lambda c: (0, 0))],
        out_shape=[jax.ShapeDtypeStruct((n_seq, s_len, d_ssm), BF16),
                   jax.ShapeDtypeStruct((nrow, sh), F32),
                   jax.ShapeDtypeStruct((nrow, sh), F32)],
        scratch_shapes=[pltpu.VMEM((nslab, nrow * pitch, LANES), F32),
                        pltpu.VMEM((nslab, nrow * pitch, LANES), F32),
                        pltpu.VMEM((2, nslab, nrow, LANES), F32)],
        compiler_params=_cparams(1),
        name="s5_prompt",
    )(u, bb, cc, a, d)

    def unpack(h):
        return h.reshape(2, n_seq, g // 2, p).transpose(1, 0, 2, 3).reshape(n_seq, g, p)

    return z, unpack(hre), unpack(him)


def _s5_step_kernel(u_ref, h0r_ref, h0i_ref, bb_ref, cc_ref, ar_ref, ai_ref, d_ref, z_ref, hr_ref, hi_ref):
    _, nblk, cb, sb = bb_ref.shape
    u = u_ref[...]
    ub = u.astype(BF16)
    for b in range(nblk):
        lhs = ub[:, b * cb:(b + 1) * cb]
        st = slice(b * sb, (b + 1) * sb)
        x_re = jnp.dot(lhs, bb_ref[0, b], preferred_element_type=F32)
        x_im = jnp.dot(lhs, bb_ref[1, b], preferred_element_type=F32)
        a_re, a_im = ar_ref[:, st], ai_ref[:, st]
        h0r, h0i = h0r_ref[:, st], h0i_ref[:, st]
        h_re = a_re * h0r - a_im * h0i + x_re
        h_im = a_re * h0i + a_im * h0r + x_im
        hr_ref[:, st] = h_re
        hi_ref[:, st] = h_im
        lhs2 = jnp.concatenate([h_re, h_im], axis=1).astype(BF16)
        y = jnp.dot(lhs2, cc_ref[b], preferred_element_type=F32)
        y = y + d_ref[:, b * cb:(b + 1) * cb] * u[:, b * cb:(b + 1) * cb]
        z_ref[:, b * cb:(b + 1) * cb] = _gelu_tanh(y).astype(z_ref.dtype)


def _s5_sample(u, h0_re, h0_im, s5p):
    a_re, a_im, bb_re, bb_im, c_re, c_im, d_skip = s5p
    g, p, gc = bb_re.shape
    db = u.shape[0]
    bb, cc, _, d, _ = _s5_layout(a_re, a_im, bb_re, bb_im, c_re, c_im, d_skip, SUBLANES)
    ns = g * p
    z, hr, hi = pl.pallas_call(
        _s5_step_kernel,
        out_shape=[jax.ShapeDtypeStruct((db, g * gc), BF16),
                   jax.ShapeDtypeStruct((db, ns), F32),
                   jax.ShapeDtypeStruct((db, ns), F32)],
        compiler_params=pltpu.CompilerParams(vmem_limit_bytes=VMEM_LIMIT_BYTES),
        name="s5_sample",
    )(u, h0_re.reshape(db, ns), h0_im.reshape(db, ns), bb, cc,
      a_re.reshape(1, ns), a_im.reshape(1, ns), d)
    return z, hr.reshape(db, g, p), hi.reshape(db, g, p)


def _mix_kernel(z_ref, at_ref, ga_ref, gb_ref, x_ref, wv_ref, wg_ref, wup_ref, wo_ref, g1_ref, b1_ref,
                wr_ref, br_ref, h_ref, route_ref, *, alpha, n_exp):
    z = z_ref[...]
    br_ssm = (jnp.dot(z, wv_ref[...], preferred_element_type=F32)
              * _sigmoid(jnp.dot(z, wg_ref[...], preferred_element_type=F32)))
    br_attn = jnp.dot(at_ref[...].astype(BF16), wup_ref[...], preferred_element_type=F32)
    mix = _sigmoid(ga_ref[...]) * br_ssm + _sigmoid(gb_ref[...]) * br_attn
    r = alpha * x_ref[...] + jnp.dot(mix.astype(BF16), wo_ref[...], preferred_element_type=F32)
    h = _layernorm(r, g1_ref[...], b1_ref[...])
    h_ref[...] = h
    logits = jnp.dot(h.astype(BF16), wr_ref[...], preferred_element_type=F32) + br_ref[...]
    lane = lax.broadcasted_iota(I32, logits.shape, 1)
    lg = jnp.where(lane < n_exp, logits, -jnp.inf)
    vals, ids = [], []
    for _ in range(TOP_K):
        m = jnp.max(lg, axis=1, keepdims=True)
        idx = jnp.min(jnp.where(lg == m, lane, LANES), axis=1, keepdims=True)
        vals.append(m)
        ids.append(idx)
        lg = jnp.where(lane == idx, -jnp.inf, lg)
    ex = [jnp.exp(v - vals[0]) for v in vals]
    den = ex[0]
    for e in ex[1:]:
        den = den + e
    route = jnp.zeros(logits.shape, F32)
    for kk in range(TOP_K):
        route = jnp.where(lane == kk, ids[kk].astype(F32), route)
        route = jnp.where(lane == TOP_K + kk, ex[kk] / den, route)
    route_ref[...] = route


def _mix(z, attn, ga, gb, x, wts, alpha, n_exp, tm):
    t, d = x.shape
    tm = min(tm, t)
    wv, wg, wup, wo, g1, b1, wr, br = wts

    def rows(w):
        return pl.BlockSpec((tm, w), lambda i: (i, 0))

    def whole(a):
        return pl.BlockSpec(a.shape, lambda i: (0,) * a.ndim, pipeline_mode=pl.Buffered(1))

    return pl.pallas_call(
        functools.partial(_mix_kernel, alpha=alpha, n_exp=n_exp),
        grid=(t // tm,),
        in_specs=[rows(z.shape[1]), rows(attn.shape[1]), rows(d), rows(d), rows(d)] + [whole(w) for w in wts],
        out_specs=[rows(d), rows(LANES)],
        out_shape=[jax.ShapeDtypeStruct((t, d), F32), jax.ShapeDtypeStruct((t, LANES), F32)],
        compiler_params=_cparams(1),
        name="mix_ln1_router",
    )(z, attn, ga, gb, x, *wts)


def _rank_kernel(r_ref, rank_ref, cnt_ref, carry):
    @pl.when(pl.program_id(0) == 0)
    def _():
        carry[...] = jnp.zeros_like(carry)

    r = r_ref[...]
    tm = r.shape[0]
    lane = lax.broadcasted_iota(I32, r.shape, 1)
    ids = [r[:, kk:kk + 1].astype(I32) for kk in range(TOP_K)]
    onehot = jnp.zeros(r.shape, F32)
    for e in ids:
        onehot = onehot + (lane == e).astype(F32)
    tri = (lax.broadcasted_iota(I32, (tm, tm), 0) > lax.broadcasted_iota(I32, (tm, tm), 1)).astype(BF16)
    before = jnp.dot(tri, onehot.astype(BF16), preferred_element_type=F32) + carry[...]
    out = jnp.zeros(r.shape, F32)
    for kk, e in enumerate(ids):
        rk = jnp.sum(jnp.where(lane == e, before, 0.0), axis=1, keepdims=True)
        out = jnp.where(lane == kk, rk, out)
    rank_ref[...] = out.astype(I32)
    carry[...] = carry[...] + jnp.sum(onehot, axis=0, keepdims=True)
    cnt_ref[...] = carry[...]


def _ranks(route_all, tm):
    t = route_all.shape[0]
    return pl.pallas_call(
        _rank_kernel,
        grid=(t // tm,),
        in_specs=[pl.BlockSpec((tm, LANES), lambda i: (i, 0))],
        out_specs=[pl.BlockSpec((tm, LANES), lambda i: (i, 0)), pl.BlockSpec((1, LANES), lambda i: (0, 0))],
        out_shape=[jax.ShapeDtypeStruct((t, LANES), I32), jax.ShapeDtypeStruct((1, LANES), F32)],
        scratch_shapes=[pltpu.VMEM((1, LANES), F32)],
        compiler_params=_cparams(1),
        name="moe_rank",
    )(route_all)


def _dispatch_kernel(zflag_ref, dest1_ref, dest2_ref, h1_ref, h2_ref, xs_ref, zero_ref, sem, *, n1):
    i = pl.program_id(0)

    def scatter(h_ref, dest_ref):
        def row_copy(r, d):
            return pltpu.make_async_copy(h_ref.at[pl.ds(r, 1), :], xs_ref.at[pl.ds(d, 1), :], sem)

        def issue(r, _):
            for kk in range(TOP_K):
                row_copy(r, dest_ref[r * TOP_K + kk]).start(priority=kk % DMA_PRIORITIES)
            return 0

        def drain(r, _):
            for _kk in range(TOP_K):
                row_copy(0, 0).wait()
            return 0

        lax.fori_loop(0, h_ref.shape[0], issue, 0)
        lax.fori_loop(0, h_ref.shape[0], drain, 0)

    @pl.when(i == 0)
    def _():
        zero_ref[...] = jnp.zeros_like(zero_ref)

        def zero_block(j):
            r0 = pl.multiple_of(j * MOE_BLOCK, MOE_BLOCK)
            return pltpu.make_async_copy(zero_ref, xs_ref.at[pl.ds(r0, MOE_BLOCK), :], sem)

        def issue(j, _):
            @pl.when(zflag_ref[j] != 0)
            def _():
                zero_block(j).start()
            return 0

        def drain(j, _):
            @pl.when(zflag_ref[j] != 0)
            def _():
                zero_block(0).wait()
            return 0

        lax.fori_loop(0, zflag_ref.shape[0], issue, 0)
        lax.fori_loop(0, zflag_ref.shape[0], drain, 0)

    @pl.when(i < n1)
    def _():
        scatter(h1_ref, dest1_ref)

    @pl.when(i == n1)
    def _():
        scatter(h2_ref, dest2_ref)


def _dispatch(h1, h2, dest1, dest2, zflag, tm):
    t1, d = h1.shape
    n1 = t1 // tm
    per_step = tm * TOP_K
    assert t1 % tm == 0
    smem = pltpu.SMEM
    return pl.pallas_call(
        functools.partial(_dispatch_kernel, n1=n1),
        grid=(n1 + 1,),
        in_specs=[pl.BlockSpec(zflag.shape, lambda i: (0,), memory_space=smem),
                  pl.BlockSpec((per_step,), lambda i: (jnp.minimum(i, n1 - 1),), memory_space=smem),
                  pl.BlockSpec(dest2.shape, lambda i: (0,), memory_space=smem),
                  pl.BlockSpec((tm, d), lambda i: (jnp.minimum(i, n1 - 1), 0)),
                  pl.BlockSpec(h2.shape, lambda i: (0, 0))],
        out_specs=pl.BlockSpec(memory_space=pl.ANY),
        out_shape=jax.ShapeDtypeStruct((zflag.shape[0] * MOE_BLOCK, d), F32),
        scratch_shapes=[pltpu.VMEM((MOE_BLOCK, d), F32), pltpu.SemaphoreType.DMA(())],
        compiler_params=_cparams(1),
        name="moe_dispatch",
    )(zflag, dest1, dest2, h1, h2)


def _expert_weights(be_ref, nu, i, f, nf, fetch, land, wb):
    e = be_ref[i]
    first = (i == 0) | (e != be_ref[jnp.maximum(i - 1, 0)])

    @pl.when(first)
    def _():
        @pl.when((i == 0) & (f == 0))
        def _():
            for cp in fetch(e, f):
                cp.start()

        for cp in fetch(e, f):
            cp.wait()
        for m in range(land.shape[0]):
            wb[m] = land[m].astype(BF16)

        last = be_ref.shape[0] - 1
        nxt = lax.while_loop(lambda j: (j < nu) & (be_ref[jnp.minimum(j, last)] == e), lambda j: j + 1, i + 1)

        @pl.when(nxt < nu)
        def _():
            for cp in fetch(be_ref[jnp.minimum(nxt, last)], f):
                cp.start()

        @pl.when((nxt >= nu) & (f + 1 < nf))
        def _():
            for cp in fetch(be_ref[0], f + 1):
                cp.start()


def _g1_kernel(be_ref, nu_ref, xs_ref, bg_ref, bu_ref, wg_hbm, wu_hbm, act_ref, land, wb, sem):
    f = pl.program_id(0)
    i = pl.program_id(1)
    nu = nu_ref[0]
    tf = land.shape[2]

    def fetch(e, ff):
        c0 = pl.multiple_of(ff * tf, tf)
        return (pltpu.make_async_copy(wg_hbm.at[e, :, pl.ds(c0, tf)], land.at[0], sem.at[0]),
                pltpu.make_async_copy(wu_hbm.at[e, :, pl.ds(c0, tf)], land.at[1], sem.at[1]))

    @pl.when(i < nu)
    def _():
        _expert_weights(be_ref, nu, i, f, pl.num_programs(0), fetch, land, wb)
        x = xs_ref[...].astype(BF16)
        g = jnp.dot(x, wb[0], preferred_element_type=F32) + bg_ref[0]
        up = jnp.dot(x, wb[1], preferred_element_type=F32) + bu_ref[0]
        g = jnp.minimum(g, SWIGLU_LIMIT)
        up = jnp.clip(up, -SWIGLU_LIMIT, SWIGLU_LIMIT)
        act_ref[...] = ((up + 1.0) * (g * _sigmoid(SWIGLU_ALPHA * g))).astype(act_ref.dtype)

    @pl.when(i >= nu)
    def _():
        act_ref[...] = jnp.zeros_like(act_ref)


def _g2_kernel(be_ref, nu_ref, act_ref, bd_ref, wd_hbm, y_ref, land, wb, sem):
    i = pl.program_id(0)
    nu = nu_ref[0]

    def fetch(e, ff):
        del ff
        return (pltpu.make_async_copy(wd_hbm.at[e], land.at[0], sem.at[0]),)

    @pl.when(i < nu)
    def _():
        _expert_weights(be_ref, nu, i, 0, 1, fetch, land, wb)
        y_ref[...] = jnp.dot(act_ref[...], wb[0], preferred_element_type=F32) + bd_ref[0]

    @pl.when(i >= nu)
    def _():
        y_ref[...] = jnp.zeros_like(y_ref)


def _experts(xs, blk_e, n_used, w_gate, b_gate, w_up, b_up, w_down, b_down):
    n_rows, d = xs.shape
    n_exp, _, dff = w_gate.shape
    nblk = n_rows // MOE_BLOCK
    tf = min(1024, dff)
    nf = dff // tf

    def blk_row(f, i, be, nu):
        return (jnp.minimum(i, nu[0] - 1), 0)

    def b_spec(f, i, be, nu):
        return (be[jnp.minimum(i, nu[0] - 1)], 0, f)

    hbm = pl.BlockSpec(memory_space=pl.ANY)
    act = pl.pallas_call(
        _g1_kernel,
        grid_spec=pltpu.PrefetchScalarGridSpec(
            num_scalar_prefetch=2, grid=(nf, nblk),
            in_specs=[pl.BlockSpec((MOE_BLOCK, d), blk_row),
                      pl.BlockSpec((1, 1, tf), b_spec), pl.BlockSpec((1, 1, tf), b_spec), hbm, hbm],
            out_specs=pl.BlockSpec((MOE_BLOCK, tf), lambda f, i, be, nu: (i, f)),
            scratch_shapes=[pltpu.VMEM((2, d, tf), F32), pltpu.VMEM((2, d, tf), BF16),
                            pltpu.SemaphoreType.DMA((2,))]),
        out_shape=jax.ShapeDtypeStruct((n_rows, dff), BF16),
        compiler_params=_cparams(2),
        name="moe_gate_up",
    )(blk_e, n_used, xs, b_gate.reshape(n_exp, 1, dff), b_up.reshape(n_exp, 1, dff), w_gate, w_up)

    return pl.pallas_call(
        _g2_kernel,
        grid_spec=pltpu.PrefetchScalarGridSpec(
            num_scalar_prefetch=2, grid=(nblk,),
            in_specs=[pl.BlockSpec((MOE_BLOCK, dff), lambda i, be, nu: (jnp.minimum(i, nu[0] - 1), 0)),
                      pl.BlockSpec((1, 1, d), lambda i, be, nu: (be[jnp.minimum(i, nu[0] - 1)], 0, 0)), hbm],
            out_specs=pl.BlockSpec((MOE_BLOCK, d), lambda i, be, nu: (i, 0)),
            scratch_shapes=[pltpu.VMEM((1, dff, d), F32), pltpu.VMEM((1, dff, d), BF16),
                            pltpu.SemaphoreType.DMA((1,))]),
        out_shape=jax.ShapeDtypeStruct((n_rows, d), F32),
        compiler_params=_cparams(1),
        name="moe_down",
    )(blk_e, n_used, act, b_down.reshape(n_exp, 1, d), w_down)


def _combine_kernel(dest_ref, h_ref, route_ref, g_ref, b_ref, yb_ref, o_ref, buf, sem, *, alpha):
    tm = h_ref.shape[0]

    def row_copy(r, kk, d):
        return pltpu.make_async_copy(yb_ref.at[pl.ds(d, 1), :], buf.at[kk, pl.ds(r, 1), :], sem)

    def issue(r, _):
        for kk in range(TOP_K):
            row_copy(r, kk, dest_ref[r * TOP_K + kk]).start(priority=kk % DMA_PRIORITIES)
        return 0

    def drain(r, _):
        for kk in range(TOP_K):
            row_copy(0, kk, 0).wait()
        return 0

    lax.fori_loop(0, tm, issue, 0)
    lax.fori_loop(0, tm, drain, 0)
    route = route_ref[...]
    acc = alpha * h_ref[...]
    for kk in range(TOP_K):
        acc = acc + route[:, TOP_K + kk:TOP_K + kk + 1] * buf[kk]
    o_ref[...] = _layernorm(acc, g_ref[...], b_ref[...])


def _combine(h, route, dest, yb, g2, b2, alpha, tm):
    t, d = h.shape
    tm = min(tm, t)
    return pl.pallas_call(
        functools.partial(_combine_kernel, alpha=alpha),
        grid=(t // tm,),
        in_specs=[pl.BlockSpec((tm * TOP_K,), lambda i: (i,), memory_space=pltpu.SMEM),
                  pl.BlockSpec((tm, d), lambda i: (i, 0)),
                  pl.BlockSpec((tm, LANES), lambda i: (i, 0)),
                  pl.BlockSpec((1, d), lambda i: (0, 0)),
                  pl.BlockSpec((1, d), lambda i: (0, 0)),
                  pl.BlockSpec(memory_space=pl.ANY)],
        out_specs=pl.BlockSpec((tm, d), lambda i: (i, 0)),
        out_shape=jax.ShapeDtypeStruct((t, d), F32),
        scratch_shapes=[pltpu.VMEM((TOP_K, tm, d), F32), pltpu.SemaphoreType.DMA(())],
        compiler_params=_cparams(1),
        name="moe_combine_ln2",
    )(dest, h, route, g2, b2, yb)


def _layer(x_prompt, x_sample, cache_k, cache_v, page_table, st_re, st_im, w_in,
           lam_re, lam_im, log_dt, b_re, b_im, c_re, c_im, d_skip,
           w_glu_val, w_glu_gate, w_attn_up, w_o, ln1_g, ln1_b,
           w_router, b_router, w_gate, b_gate, w_up, b_up, w_down, b_down, ln2_g, ln2_b, depth):
    n_seq, s_len, d_model = x_prompt.shape
    db, dec_seq, _ = x_sample.shape
    assert dec_seq == 1, "decode path handles one new token per sequence"
    n_phys, page, n_heads, hd = cache_k.shape
    d_attn = n_heads * hd
    n_groups, n_state = st_re.shape[1:]
    d_ssm = n_groups * b_re.shape[-1]
    n_exp = w_router.shape[1]
    alpha = (2 * depth) ** 0.25
    tp, ts = n_seq * s_len, db
    past = page_table.shape[1] * page
    col_q, col_k, col_v, col_u = 0, d_attn, 2 * d_attn, 3 * d_attn
    col_ga, col_gb = 3 * d_attn + d_ssm, 3 * d_attn + d_ssm + d_model

    wb = w_in.astype(BF16)
    s5p = _s5_discretise(lam_re.astype(F32), lam_im.astype(F32), log_dt, b_re.astype(F32), b_im.astype(F32))
    s5p = s5p + (c_re.astype(F32), c_im.astype(F32), d_skip)
    wr_pad = jnp.zeros((d_model, LANES), BF16).at[:, :n_exp].set(w_router.astype(BF16))
    br_pad = jnp.zeros((1, LANES), F32).at[0, :n_exp].set(b_router.astype(F32))
    mix_w = (w_glu_val.astype(BF16), w_glu_gate.astype(BF16), w_attn_up.astype(BF16), w_o.astype(BF16),
             ln1_g.reshape(1, d_model).astype(F32), ln1_b.reshape(1, d_model).astype(F32), wr_pad, br_pad)

    def project(x2, pos, reps, tm, tag):
        xb = x2.astype(BF16)
        tabs = tuple(jnp.tile(tab, (reps, 1)) for tab in _rope_tables(pos, hd))
        kw = dict(tm=tm, tn=1024)
        q = _proj(xb, wb, col_q, d_attn, BF16 if tag == "p" else F32, rope_tabs=tabs, hd=hd, name=f"proj_q_{tag}", **kw)
        k = _proj(xb, wb, col_k, d_attn, F32, rope_tabs=tabs, hd=hd, name=f"proj_k_{tag}", **kw)
        v = _proj(xb, wb, col_v, d_attn, F32, name=f"proj_v_{tag}", **kw)
        u = _proj(xb, wb, col_u, d_ssm, F32, name=f"proj_u_{tag}", **kw)
        ga = _proj(xb, wb, col_ga, d_model, F32, name=f"proj_ga_{tag}", **kw)
        gb = _proj(xb, wb, col_gb, d_model, F32, name=f"proj_gb_{tag}", **kw)
        return q, k, v, u, ga, gb

    xp2 = x_prompt.reshape(tp, d_model)
    q1, k1, v1, u1, ga1, gb1 = project(xp2, jnp.arange(s_len), n_seq, 1024, "p")
    attn1, ksum = _moba_prefill(q1, k1, v1, n_seq, s_len, n_heads, hd, page_table, cache_k)
    z1, hr1, hi1 = _s5_prompt(u1.reshape(n_seq, s_len, d_ssm), s5p, n_seq, s_len)
    h1, route1 = _mix(z1.reshape(tp, d_ssm), attn1, ga1, gb1, xp2, mix_w, alpha, n_exp, 256)

    xs2 = x_sample.reshape(ts, d_model)
    q2, k2, v2, u2, ga2, gb2 = project(xs2, jnp.full((1,), past, I32), ts, ts, "s")
    attn2 = _moba_decode(q2.reshape(ts, n_heads, hd), k2.reshape(ts, n_heads, hd), v2.reshape(ts, n_heads, hd),
                         cache_k, cache_v, page_table, ksum)
    z2, hr2, hi2 = _s5_sample(u2, st_re, st_im, s5p)
    h2, route2 = _mix(z2, attn2.reshape(ts, d_attn), ga2, gb2, xs2, mix_w, alpha, n_exp, ts)

    tr = 256
    t_all = tp + ts
    t_pad = -(-t_all // tr) * tr
    route_all = jnp.concatenate([route1, route2, jnp.full((t_pad - t_all, LANES), -1.0, F32)], axis=0)
    rank_all, cnt = _ranks(route_all, tr)
    counts = cnt[0, :n_exp].astype(I32)
    padded = (counts + MOE_BLOCK - 1) // MOE_BLOCK * MOE_BLOCK
    pad_end = jnp.cumsum(padded)
    pad_start = pad_end - padded
    ids = route_all[:t_all, :TOP_K].astype(I32)
    dest = (pad_start[ids] + rank_all[:t_all, :TOP_K]).astype(I32)
    nblk = -(-(t_all * TOP_K) // MOE_BLOCK) + n_exp
    blk_first_row = jnp.arange(nblk, dtype=I32) * MOE_BLOCK
    blk_e = jnp.minimum(jnp.sum(pad_end[None, :] <= blk_first_row[:, None], axis=1), n_exp - 1).astype(I32)
    n_used = (pad_end[-1:] // MOE_BLOCK).astype(I32)
    dest1 = dest[:tp].reshape(-1)
    dest2 = dest[tp:].reshape(-1)

    blk_idx = jnp.arange(nblk, dtype=I32)
    next_e = jnp.concatenate([blk_e[1:], jnp.full((1,), -1, I32)])
    zflag = ((blk_idx >= n_used[0] - 1) | (blk_e != next_e)).astype(I32)
    xs = _dispatch(h1, h2, dest1, dest2, zflag, 256)
    yb = _experts(xs, blk_e, n_used, w_gate, b_gate, w_up, b_up, w_down, b_down)
    g2 = ln2_g.reshape(1, d_model).astype(F32)
    b2 = ln2_b.reshape(1, d_model).astype(F32)
    y1 = _combine(h1, route1, dest1, yb, g2, b2, alpha, 256)
    y2 = _combine(h2, route2, dest2, yb, g2, b2, alpha, ts)

    return (y1.reshape(n_seq, s_len, d_model), y2.reshape(db, 1, d_model),
            k1.reshape(n_seq, s_len, n_heads, hd), v1.reshape(n_seq, s_len, n_heads, hd), hr1, hi1,
            k2.reshape(db, 1, n_heads, hd), v2.reshape(db, 1, n_heads, hd), hr2, hi2)


def kernel(x_prompt, x_sample, cache_k, cache_v, page_table, state_ssm_re, state_ssm_im, w_in, ssm_lambda_re, ssm_lambda_im, ssm_log_dt, ssm_b_re, ssm_b_im, ssm_c_re, ssm_c_im, ssm_d, w_glu_val, w_glu_gate, w_attn_up, w_o, ln1_g, ln1_b, w_router, b_router, w_gate, b_gate, w_up, b_up, w_down, b_down, ln2_g, ln2_b):
    depth = w_in.shape[0]
    assert depth == 1, "single-layer step"
    outs = _layer(x_prompt, x_sample, cache_k[0], cache_v[0], page_table, state_ssm_re[0], state_ssm_im[0],
                  w_in[0], ssm_lambda_re[0], ssm_lambda_im[0], ssm_log_dt[0], ssm_b_re[0], ssm_b_im[0],
                  ssm_c_re[0], ssm_c_im[0], ssm_d[0], w_glu_val[0], w_glu_gate[0], w_attn_up[0], w_o[0],
                  ln1_g[0], ln1_b[0], w_router[0], b_router[0], w_gate[0], b_gate[0], w_up[0], b_up[0],
                  w_down[0], b_down[0], ln2_g[0], ln2_b[0], depth)
    y1, y2, k1, v1, hr1, hi1, k2, v2, hr2, hi2 = outs
    return (y1, y2, k1[None], v1[None], hr1[None], hi1[None], k2[None], v2[None], hr2[None], hi2[None])
```
